```python
import math
import jax
import jax.numpy as jnp
from jax import lax
import numpy as np

D_MODEL = 2048
BATCH = 1
SEQ = 16384
DEPTH = 4

GRID_W = 64
CTX_LEN = 256

POOL_GROUPS = 4
POOL_WINDOWS = (2, 4, 8, 16)
POOL_WIDTH = D_MODEL // 2
POOL_GROUP_DIM = POOL_WIDTH // POOL_GROUPS
SGU_WIDTH = D_MODEL // 2
SGU_CHUNK = 128
SGU_HEADS = 8
SGU_HEAD_DIM = SGU_WIDTH // SGU_HEADS
EVEN_IN = POOL_WIDTH + 2 * SGU_WIDTH
EVEN_OUT = POOL_WIDTH + SGU_WIDTH

DIFF_HEADS = 4
DIFF_QK_DIM = 128
DIFF_V_DIM = 2 * DIFF_QK_DIM
DIFF_WIDTH = DIFF_HEADS * DIFF_V_DIM
CONV_WIDTH = D_MODEL // 2
CONV_K = 3
ODD_IN = 3 * DIFF_WIDTH + 3 * CONV_WIDTH
ODD_OUT = DIFF_WIDTH + CONV_WIDTH
Q_BLOCK = 128
ROPE_BASE = 10000.0

N_EXPERTS = 64
EXPERT_FF = 384
SHARED_FF = 384
TOP_K = 8
N_GROUPS = 8
TOPK_GROUPS = 4
ROUTED_SCALE = 2.5
MOE_BLOCK = 256

ALPHA = (2 * DEPTH) ** 0.25
BETA = (8 * DEPTH) ** -0.25
LN_EPS = 1e-6
RMS_EPS = 1e-5
N_EVEN = (DEPTH + 1) // 2
N_ODD = DEPTH // 2

kernel_name = 'hybrid_pool_sgu_diffattn_conv_moe_dit'


def layer_norm(x, g, b):
    xf = x.astype(jnp.float32)
    mu = jnp.mean(xf, axis=-1, keepdims=True)
    var = jnp.mean(jnp.square(xf - mu), axis=-1, keepdims=True)
    out = (xf - mu) * lax.rsqrt(var + LN_EPS) * g.astype(jnp.float32) + b.astype(jnp.float32)
    return out.astype(x.dtype)


def rms_norm(x, g):
    xf = x.astype(jnp.float32)
    out = xf * lax.rsqrt(jnp.mean(jnp.square(xf), axis=-1, keepdims=True) + RMS_EPS) * g.astype(jnp.float32)
    return out.astype(x.dtype)


def centred_mean_minus_self(z, window):
    n = z.shape[1]
    zf = z.astype(jnp.float32)
    csum = jnp.concatenate([jnp.zeros_like(zf[:, :1]), jnp.cumsum(zf, axis=1)], axis=1)
    t = jnp.arange(n)
    lo = jnp.maximum(t - window // 2, 0)
    hi = jnp.minimum(t + (window - 1 - window // 2), n - 1)
    total = csum[:, hi + 1] - csum[:, lo]
    cnt = (hi - lo + 1).astype(jnp.float32)[None, :, None]
    return (total / cnt - zf).astype(z.dtype)


def multiscale_pool(z, pool_w, pool_scale):
    bsz, n, _ = z.shape
    zg = z.reshape(bsz, n, POOL_GROUPS, POOL_GROUP_DIM)
    pooled = jnp.stack([centred_mean_minus_self(zg[:, :, g], POOL_WINDOWS[g]) for g in range(POOL_GROUPS)], axis=2)
    out = jnp.einsum('bngc,gcd->bngd', pooled, pool_w) * pool_scale
    return out.reshape(bsz, n, POOL_WIDTH)


def spatial_gating(z, ln_g, ln_b, sgu_w, sgu_b):
    bsz, n, _ = z.shape
    z = jax.nn.gelu(z)
    u, v = z[..., :SGU_WIDTH], z[..., SGU_WIDTH:]
    v = layer_norm(v, ln_g, ln_b)
    vc = v.reshape(bsz, n // SGU_CHUNK, SGU_CHUNK, SGU_HEADS, SGU_HEAD_DIM)
    mixed = jnp.einsum('hpq,bnqhc->bnphc', sgu_w, vc) + sgu_b.T[:, :, None]
    return u * mixed.reshape(bsz, n, SGU_WIDTH)


def pool_sgu_mix(h, w_in, w_out, pool_w, pool_scale, sgu_ln_g, sgu_ln_b, sgu_w, sgu_b):
    p = h @ w_in
    a = multiscale_pool(p[..., :POOL_WIDTH], pool_w, pool_scale)
    s = spatial_gating(p[..., POOL_WIDTH:], sgu_ln_g, sgu_ln_b, sgu_w, sgu_b)
    return jnp.concatenate([a, s], axis=-1) @ w_out


def axial_rope_tables(n_tokens):
    rows = n_tokens // GRID_W
    r, cidx = jnp.meshgrid(jnp.arange(rows), jnp.arange(GRID_W), indexing='ij')
    pos = jnp.stack([r.reshape(-1), cidx.reshape(-1)], axis=-1).astype(jnp.float32)
    nf = DIFF_QK_DIM // 4
    inv = ROPE_BASE ** (-jnp.arange(nf, dtype=jnp.float32) / nf)
    ang = jnp.broadcast_to(pos[:, :, None, None] * inv, (n_tokens, 2, 2, nf)).reshape(n_tokens, DIFF_QK_DIM)
    return jnp.cos(ang), jnp.sin(ang)


def apply_axial_rope(x, cos, sin):
    shp = x.shape
    xs = x.reshape(shp[:-1] + (2, 2, shp[-1] // 4))
    rot = jnp.stack([-xs[..., 1, :], xs[..., 0, :]], axis=-2).reshape(shp)
    bshape = (1, cos.shape[0]) + (1,) * (x.ndim - 3) + (cos.shape[1],)
    out = x.astype(jnp.float32) * cos.reshape(bshape) + rot.astype(jnp.float32) * sin.reshape(bshape)
    return out.astype(x.dtype)


def split_qk(p):
    return p.reshape(p.shape[:2] + (DIFF_HEADS, 2, DIFF_QK_DIM))


def split_v(p):
    return p.reshape(p.shape[:2] + (DIFF_HEADS, DIFF_V_DIM))


def diff_attention(q, k, v, lam):
    bsz, lq = q.shape[:2]
    nb = lq // Q_BLOCK
    qb = jnp.moveaxis(q.reshape((bsz, nb, Q_BLOCK) + q.shape[2:]), 1, 0)

    def block(qi):
        s = jnp.einsum('bqhmd,bkhmd->bhmqk', qi, k, preferred_element_type=jnp.float32)
        p = jax.nn.softmax(s, axis=-1)
        a = p[:, :, 0] - lam * p[:, :, 1]
        return jnp.einsum('bhqk,bkhd->bqhd', a.astype(v.dtype), v)

    o = lax.map(block, qb)
    return jnp.moveaxis(o, 0, 1).reshape((bsz, lq) + v.shape[2:])


def short_gated_conv(g, conv_w):
    xin, gate_b, gate_c = jnp.split(g, 3, axis=-1)
    z = lax.conv_general_dilated(gate_c * xin, conv_w[:, None, :], window_strides=(1,),
                                 padding=[(CONV_K // 2, CONV_K // 2)],
                                 dimension_numbers=('NWC', 'WIO', 'NWC'),
                                 feature_group_count=CONV_WIDTH)
    return gate_b * z


def diff_conv_mix(h, hc, w_in, w_out, diff_lambda, diff_subln, conv_w, cos, sin, layer_idx, need_ctx):
    W = DIFF_WIDTH
    lam_init = 0.8 - 0.6 * math.exp(-0.3 * layer_idx)
    lf = diff_lambda.astype(jnp.float32)
    lam = jnp.exp(jnp.sum(lf[0] * lf[1])) - jnp.exp(jnp.sum(lf[2] * lf[3])) + lam_init
    q_scale = DIFF_QK_DIM ** -0.5

    p = h @ w_in
    q = apply_axial_rope(split_qk(p[..., :W]), cos, sin) * q_scale
    k = apply_axial_rope(split_qk(p[..., W:2 * W]), cos, sin)
    v = split_v(p[..., 2 * W:3 * W])
    pc = hc @ (w_in if need_ctx else w_in[:, W:3 * W])
    kv_c = pc[..., W:3 * W] if need_ctx else pc
    kc = split_qk(kv_c[..., :W])
    vc = split_v(kv_c[..., W:])

    def head_out(o):
        return (rms_norm(o, diff_subln) * (1 - lam_init)).reshape(o.shape[:2] + (W,))

    o = diff_attention(q, jnp.concatenate([k, kc], axis=1), jnp.concatenate([v, vc], axis=1), lam)
    y = jnp.concatenate([head_out(o), short_gated_conv(p[..., 3 * W:], conv_w)], axis=-1) @ w_out
    if not need_ctx:
        return y, None
    oc = diff_attention(split_qk(pc[..., :W]) * q_scale, kc, vc, lam)
    yc = jnp.concatenate([head_out(oc), short_gated_conv(pc[..., 3 * W:], conv_w)], axis=-1) @ w_out
    return y, yc


def swiglu(h, wg, wu, wd):
    return (jax.nn.silu(h @ wg) * (h @ wu)) @ wd


def moe(h, router_w, router_bias, w_gate, w_up, w_down, ws_gate, ws_up, ws_down):
    n, d = h.shape
    scores = jax.nn.sigmoid((h @ router_w).astype(jnp.float32))
    biased = scores + router_bias.astype(jnp.float32)
    per_group = N_EXPERTS // N_GROUPS
    grp_score = lax.top_k(biased.reshape(n, N_GROUPS, per_group), 2)[0].sum(-1)
    _, gsel = lax.top_k(grp_score, TOPK_GROUPS)
    gmask = jnp.any(gsel[:, :, None] == jnp.arange(N_GROUPS), axis=1)
    masked = jnp.where(jnp.repeat(gmask, per_group, axis=1), biased, -jnp.inf)
    _, eidx = lax.top_k(masked, TOP_K)
    gates = jnp.take_along_axis(scores, eidx, axis=1)
    gates = gates / jnp.sum(gates, axis=-1, keepdims=True) * ROUTED_SCALE

    nk = n * TOP_K
    flat_e = eidx.reshape(-1)
    flat_w = gates.reshape(-1).astype(h.dtype)
    flat_tok = (jnp.arange(nk) // TOP_K).astype(jnp.int32)
    order = jnp.argsort(flat_e)
    se = flat_e[order]
    counts = jnp.bincount(flat_e, length=N_EXPERTS)
    starts = jnp.cumsum(counts) - counts
    padded = (counts + MOE_BLOCK - 1) // MOE_BLOCK * MOE_BLOCK
    pends = jnp.cumsum(padded)
    dest = pends[se] - padded[se] + jnp.arange(nk) - starts[se]
    n_blocks = -(-nk // MOE_BLOCK) + N_EXPERTS
    n_slots = n_blocks * MOE_BLOCK
    slot_tok = jnp.full((n_slots,), n, jnp.int32).at[dest].set(flat_tok[order])
    slot_w = jnp.zeros((n_slots,), h.dtype).at[dest].set(flat_w[order])
    block_e = jnp.minimum(jnp.searchsorted(pends, jnp.arange(n_blocks) * MOE_BLOCK, side='right'), N_EXPERTS - 1)

    h_pad = jnp.concatenate([h, jnp.zeros((1, d), h.dtype)], axis=0)

    def step(acc, blk):
        e, tok, w = blk
        xb = h_pad[tok]
        yb = (jax.nn.silu(xb @ w_gate[e]) * (xb @ w_up[e])) @ w_down[e]
        return acc.at[tok].add(yb * w[:, None]), None

    acc, _ = lax.scan(step, jnp.zeros_like(h_pad),
                      (block_e, slot_tok.reshape(n_blocks, MOE_BLOCK), slot_w.reshape(n_blocks, MOE_BLOCK)))
    return acc[:n] + swiglu(h, ws_gate, ws_up, ws_down)


def setup_inputs(seed: int = 0) -> dict:
    key = jax.random.key(seed)
    ks = iter(jax.random.split(key, 32))
    f32 = jnp.float32
    D = D_MODEL

    def nrm(shape, scale):
        return jax.random.normal(next(ks), shape, f32) * scale

    return {
        'x': nrm((BATCH, SEQ, D), 1.0),
        'c': nrm((BATCH, D), 1.0),
        'ctx': nrm((BATCH, CTX_LEN, D), 1.0),
        'c_ctx': nrm((D,), 1.0),
        'ada_w': nrm((DEPTH, D, 6 * D), 0.5 * D ** -0.5),
        'ada_b': nrm((DEPTH, 6 * D), 0.02),
        'ln_g': 1.0 + nrm((DEPTH, 2, D), 0.02),
        'ln_b': nrm((DEPTH, 2, D), 0.02),
        'ev_w_in': nrm((N_EVEN, D, EVEN_IN), D ** -0.5),
        'ev_w_out': nrm((N_EVEN, EVEN_OUT, D), BETA * EVEN_OUT ** -0.5),
        'pool_w': nrm((N_EVEN, POOL_GROUPS, POOL_GROUP_DIM, POOL_GROUP_DIM), POOL_GROUP_DIM ** -0.5),
        'pool_scale': 1.0 + nrm((N_EVEN, POOL_GROUPS, POOL_GROUP_DIM), 0.02),
        'sgu_ln_g': 1.0 + nrm((N_EVEN, SGU_WIDTH), 0.02),
        'sgu_ln_b': nrm((N_EVEN, SGU_WIDTH), 0.02),
        'sgu_w': nrm((N_EVEN, SGU_HEADS, SGU_CHUNK, SGU_CHUNK), SGU_CHUNK ** -0.5),
        'sgu_b': 1.0 + nrm((N_EVEN, SGU_HEADS, SGU_CHUNK), 0.02),
        'od_w_in': nrm((N_ODD, D, ODD_IN), D ** -0.5),
        'od_w_out': nrm((N_ODD, ODD_OUT, D), BETA * ODD_OUT ** -0.5),
        'diff_lambda': nrm((N_ODD, 4, DIFF_QK_DIM), 0.1),
        'diff_subln': 1.0 + nrm((N_ODD, DIFF_V_DIM), 0.02),
        'conv_w': nrm((N_ODD, CONV_K, CONV_WIDTH), CONV_K ** -0.5),
        'router_w': nrm((DEPTH, D, N_EXPERTS), D ** -0.5),
        'router_bias': nrm((DEPTH, N_EXPERTS), 0.01),
        'exp_w_gate': nrm((DEPTH, N_EXPERTS, D, EXPERT_FF), D ** -0.5),
        'exp_w_up': nrm((DEPTH, N_EXPERTS, D, EXPERT_FF), D ** -0.5),
        'exp_w_down': nrm((DEPTH, N_EXPERTS, EXPERT_FF, D), BETA * EXPERT_FF ** -0.5),
        'sh_w_gate': nrm((DEPTH, D, SHARED_FF), D ** -0.5),
        'sh_w_up': nrm((DEPTH, D, SHARED_FF), D ** -0.5),
        'sh_w_down': nrm((DEPTH, SHARED_FF, D), BETA * SHARED_FF ** -0.5),
    }


def reference(x, c, ctx, c_ctx, ada_w, ada_b, ln_g, ln_b, ev_w_in, ev_w_out, pool_w, pool_scale,
              sgu_ln_g, sgu_ln_b, sgu_w, sgu_b, od_w_in, od_w_out, diff_lambda, diff_subln, conv_w,
              router_w, router_bias, exp_w_gate, exp_w_up, exp_w_down, sh_w_gate, sh_w_up, sh_w_down):
    bsz, L, D = x.shape
    n_ctx = ctx.shape[1]
    cos, sin = axial_rope_tables(L)
    s_lat = jax.nn.silu(c)
    s_ctx = jax.nn.silu(c_ctx)
    xc = ctx
    for li in range(DEPTH):
        last = li == DEPTH - 1
        j = li // 2
        m = (s_lat @ ada_w[li] + ada_b[li]).reshape(bsz, 6, 1, D)
        mc = (s_ctx @ ada_w[li] + ada_b[li]).reshape(6, D)
        h = x * (1 + m[:, 1]) + m[:, 0]
        hc = xc * (1 + mc[1]) + mc[0]

        if li % 2 == 0:
            mix_args = (ev_w_in[j], ev_w_out[j], pool_w[j], pool_scale[j], sgu_ln_g[j], sgu_ln_b[j], sgu_w[j], sgu_b[j])
            y = pool_sgu_mix(h, *mix_args)
            yc = None if last else pool_sgu_mix(hc, *mix_args)
        else:
            y, yc = diff_conv_mix(h, hc, od_w_in[j], od_w_out[j], diff_lambda[j], diff_subln[j], conv_w[j],
                                  cos, sin, li, not last)
        x = layer_norm(ALPHA * x + m[:, 2] * y, ln_g[li, 0], ln_b[li, 0])

        h = x * (1 + m[:, 4]) + m[:, 3]
        moe_args = (router_w[li], router_bias[li], exp_w_gate[li], exp_w_up[li], exp_w_down[li],
                    sh_w_gate[li], sh_w_up[li], sh_w_down[li])
        if last:
            f = moe(h.reshape(bsz * L, D), *moe_args).reshape(bsz, L, D)
        else:
            xc = layer_norm(ALPHA * xc + mc[2] * yc, ln_g[li, 0], ln_b[li, 0])
            hc = xc * (1 + mc[4]) + mc[3]
            f_all = moe(jnp.concatenate([hc.reshape(bsz * n_ctx, D), h.reshape(bsz * L, D)], axis=0), *moe_args)
            fc = f_all[:bsz * n_ctx].reshape(bsz, n_ctx, D)
            f = f_all[bsz * n_ctx:].reshape(bsz, L, D)
            xc = layer_norm(ALPHA * xc + mc[5] * fc, ln_g[li, 1], ln_b[li, 1])
        x = layer_norm(ALPHA * x + m[:, 5] * f, ln_g[li, 1], ln_b[li, 1])
    return x
```

```python
import functools
import math

import jax
import jax.numpy as jnp
from jax import lax
from jax.experimental import pallas as pl
from jax.experimental.pallas import tpu as pltpu

F32 = jnp.float32
BF16 = jnp.bfloat16
I32 = jnp.int32

D = 2048
DEPTH = 4
GRID_W = 64

POOL_GROUPS = 4
POOL_WINDOWS = (2, 4, 8, 16)
POOL_WIDTH = 1024
POOL_GROUP_DIM = 256
SGU_WIDTH = 1024
SGU_CHUNK = 128
SGU_HEADS = 8
EVEN_IN = 3072

DIFF_HEADS = 4
DIFF_QK_DIM = 128
DIFF_V_DIM = 256
DIFF_WIDTH = 1024
CONV_WIDTH = 1024
ODD_IN = 6144
ROPE_BASE = 10000.0

N_EXPERTS = 64
EXPERT_FF = 384
TOP_K = 8
N_GROUPS = 8
GROUP_SIZE = 8
TOPK_GROUPS = 4
ROUTED_SCALE = 2.5

ALPHA = (2 * DEPTH) ** 0.25
LN_EPS = 1e-6
RMS_EPS = 1e-5

T = 256
HALO = 8
MOD_ROWS = 8
ADA_TN = 1536
ATT_TK = 512
VMEM_LIMIT = 56 * 1024 * 1024


def _cparams(sem, vmem=VMEM_LIMIT):
    return pltpu.CompilerParams(dimension_semantics=sem, vmem_limit_bytes=vmem)


def _sigmoid(x):
    return 1.0 / (1.0 + jnp.exp(-x))


def _silu(x):
    return x * _sigmoid(x)


def _layer_norm(v, g, b):
    mu = jnp.mean(v, axis=-1, keepdims=True)
    d = v - mu
    var = jnp.mean(d * d, axis=-1, keepdims=True)
    return d * lax.rsqrt(var + LN_EPS) * g + b


def _dot(a, b):
    return jnp.dot(a, b, preferred_element_type=F32)


def _dot_nt(a, b):
    return lax.dot_general(a, b, (((1,), (1,)), ((), ())), preferred_element_type=F32)


def _ada_kernel(c_ref, w_ref, b_ref, o_ref):
    tn = w_ref.shape[2]

    def body(m, carry):
        a0, a1 = carry
        r = pl.multiple_of(m * 8, 8)
        w = w_ref[0, pl.ds(r, 8), :]
        s = _silu(c_ref[pl.ds(r, 8), :])
        return a0 + w * s[:, 0:1], a1 + w * s[:, 1:2]

    z = jnp.zeros((8, tn), F32)
    a0, a1 = lax.fori_loop(0, D // 8, body, (z, z))
    bias = b_ref[0]
    o_ref[0] = jnp.zeros((8, tn), F32)
    o_ref[0, 0:1, :] = jnp.sum(a0, axis=0, keepdims=True) + bias
    o_ref[0, 1:2, :] = jnp.sum(a1, axis=0, keepdims=True) + bias


def _ada_mods(c_cols, ada_w, ada_b):
    depth = ada_w.shape[0]
    n6 = ada_w.shape[2]
    out = pl.pallas_call(
        _ada_kernel,
        grid=(depth, n6 // ADA_TN),
        in_specs=[
            pl.BlockSpec((D, 2), lambda l, j: (0, 0)),
            pl.BlockSpec((1, D, ADA_TN), lambda l, j: (l, 0, j)),
            pl.BlockSpec((1, 1, ADA_TN), lambda l, j: (l, 0, j)),
        ],
        out_specs=pl.BlockSpec((1, 8, ADA_TN), lambda l, j: (l, 0, j)),
        out_shape=jax.ShapeDtypeStruct((depth, 8, n6), F32),
        compiler_params=_cparams(("arbitrary", "arbitrary")),
        name="ada_mods",
    )(c_cols, ada_w, ada_b.reshape(depth, 1, n6))
    mods = out[:, :2, :].reshape(depth, 2, 6, D)
    return jnp.pad(mods, ((0, 0), (0, 0), (0, MOD_ROWS - 6), (0, 0)))


def _modmm_kernel(x_ref, m_ref, w_ref, o_ref, *, sh, sc):
    m = m_ref[0]
    h = x_ref[...] * (1.0 + m[sc:sc + 1, :]) + m[sh:sh + 1, :]
    o_ref[...] = _dot(h.astype(BF16), w_ref[...])


def _mod_matmul(x, mod, w_bf, nlat, *, sh, sc, tn):
    n = x.shape[0]
    nout = w_bf.shape[1]
    return pl.pallas_call(
        functools.partial(_modmm_kernel, sh=sh, sc=sc),
        grid=(nout // tn, n // T),
        in_specs=[
            pl.BlockSpec((T, D), lambda j, i: (i, 0)),
            pl.BlockSpec((1, MOD_ROWS, D), lambda j, i: (i // nlat, 0, 0)),
            pl.BlockSpec((D, tn), lambda j, i: (0, j)),
        ],
        out_specs=pl.BlockSpec((T, tn), lambda j, i: (i, j)),
        out_shape=jax.ShapeDtypeStruct((n, nout), F32),
        compiler_params=_cparams(("arbitrary", "arbitrary")),
        name="mod_matmul",
    )(x, mod, w_bf)


def _seq_flags(i, nlat, ntiles):
    is_ctx = i >= nlat
    first = jnp.logical_or(i == 0, i == nlat)
    last = jnp.logical_or(i == nlat - 1, i == ntiles - 1)
    t0 = jnp.where(is_ctx, i - nlat, i) * T
    nseq = jnp.where(is_ctx, (ntiles - nlat) * T, nlat * T)
    return first, last, t0, nseq


def _residual_ln(x, y, m, gate_row, g, b):
    return _layer_norm(ALPHA * x + m[gate_row:gate_row + 1, :] * y, g, b)


def _even_mix_kernel(pp_ref, pprev_ref, pnext_ref, pu_ref, pv_ref, x_ref, m_ref,
                     poolw_ref, pscale_ref, slng_ref, slnb_ref, sguw_ref, sgub_ref,
                     wout_ref, g_ref, b_ref, o_ref, e_ref, cat_ref, *, nlat, ntiles):
    i = pl.program_id(0)
    first, last, t0, nseq = _seq_flags(i, nlat, ntiles)

    e_ref[0:HALO, :] = jnp.where(first, 0.0, pprev_ref[...])
    e_ref[HALO:HALO + T, :] = pp_ref[...]
    e_ref[HALO + T:HALO + T + HALO, :] = jnp.where(last, 0.0, pnext_ref[...])

    pos = t0 + lax.broadcasted_iota(I32, (T, 1), 0)
    for g in range(POOL_GROUPS):
        w = POOL_WINDOWS[g]
        lo, hi = w // 2, w - 1 - w // 2
        cols = slice(g * POOL_GROUP_DIM, (g + 1) * POOL_GROUP_DIM)
        tot = e_ref[HALO - lo:HALO - lo + T, cols]
        for d in range(-lo + 1, hi + 1):
            tot = tot + e_ref[HALO + d:HALO + d + T, cols]
        cnt = jnp.minimum(pos + hi, nseq - 1) - jnp.maximum(pos - lo, 0) + 1
        pooled = tot / cnt.astype(F32) - e_ref[HALO:HALO + T, cols]
        a = _dot(pooled.astype(BF16), poolw_ref[g]) * pscale_ref[g:g + 1, :]
        cat_ref[:, cols] = a.astype(BF16)

    zu = jax.nn.gelu(pu_ref[...])
    v = _layer_norm(jax.nn.gelu(pv_ref[...]), slng_ref[...], slnb_ref[...]).astype(BF16)
    for h in range(SGU_HEADS):
        cols = slice(h * SGU_CHUNK, (h + 1) * SGU_CHUNK)
        mixed = _dot(sguw_ref[h], v[:, cols]) + sgub_ref[:, h:h + 1]
        cat_ref[:, POOL_WIDTH + h * SGU_CHUNK:POOL_WIDTH + (h + 1) * SGU_CHUNK] = (zu[:, cols] * mixed).astype(BF16)

    y = _dot(cat_ref[...], wout_ref[...])
    o_ref[...] = _residual_ln(x_ref[...], y, m_ref[0], 2, g_ref[...], b_ref[...])


def _even_mix(p, x, mod, poolw_bf, pscale, slng, slnb, sguw_bd, sgub_t, wout_bf, g, b, nlat):
    n = x.shape[0]
    ntiles = n // T
    hb = T // HALO
    nhb = n // HALO
    kern = functools.partial(_even_mix_kernel, nlat=nlat, ntiles=ntiles)
    const2 = lambda i: (0, 0)
    const3 = lambda i: (0, 0, 0)
    return pl.pallas_call(
        kern,
        grid=(ntiles,),
        in_specs=[
            pl.BlockSpec((T, POOL_WIDTH), lambda i: (i, 0)),
            pl.BlockSpec((HALO, POOL_WIDTH), lambda i: (jnp.maximum(i * hb - 1, 0), 0)),
            pl.BlockSpec((HALO, POOL_WIDTH), lambda i: (jnp.minimum((i + 1) * hb, nhb - 1), 0)),
            pl.BlockSpec((T, SGU_WIDTH), lambda i: (i, 1)),
            pl.BlockSpec((T, SGU_WIDTH), lambda i: (i, 2)),
            pl.BlockSpec((T, D), lambda i: (i, 0)),
            pl.BlockSpec((1, MOD_ROWS, D), lambda i: (i // nlat, 0, 0)),
            pl.BlockSpec((POOL_GROUPS, POOL_GROUP_DIM, POOL_GROUP_DIM), const3),
            pl.BlockSpec((POOL_GROUPS, POOL_GROUP_DIM), const2),
            pl.BlockSpec((1, SGU_WIDTH), const2),
            pl.BlockSpec((1, SGU_WIDTH), const2),
            pl.BlockSpec((SGU_HEADS, T, T), const3),
            pl.BlockSpec((T, SGU_HEADS), const2),
            pl.BlockSpec((D, D), const2),
            pl.BlockSpec((1, D), const2),
            pl.BlockSpec((1, D), const2),
        ],
        out_specs=pl.BlockSpec((T, D), lambda i: (i, 0)),
        out_shape=jax.ShapeDtypeStruct((n, D), F32),
        scratch_shapes=[pltpu.VMEM((T + 2 * HALO, POOL_WIDTH), F32), pltpu.VMEM((T, D), BF16)],
        compiler_params=_cparams(("arbitrary",)),
        name="even_mix",
    )(p, p, p, p, p, x, mod, poolw_bf, pscale, slng, slnb, sguw_bd, sgub_t, wout_bf, g, b)


def _rope_kernel(pq_ref, pk_ref, pv_ref, cos_ref, sin_ref, q_ref, k_ref, v_ref):
    cos = cos_ref[...]
    sin = sin_ref[...]
    lane = lax.broadcasted_iota(I32, (1, DIFF_QK_DIM), 1)
    low_half = (lane % 64) < 32
    q_scale = DIFF_QK_DIM ** -0.5

    def rope(x):
        rot = jnp.where(low_half, -pltpu.roll(x, 96, 1), pltpu.roll(x, 32, 1))
        return x * cos + rot * sin

    for j in range(DIFF_WIDTH // DIFF_QK_DIM):
        cols = slice(j * DIFF_QK_DIM, (j + 1) * DIFF_QK_DIM)
        q_ref[:, cols] = (rope(pq_ref[:, cols]) * q_scale).astype(BF16)
        k_ref[:, cols] = rope(pk_ref[:, cols]).astype(BF16)
    v_ref[...] = pv_ref[...].astype(BF16)


def _rope_prep(p, cos, sin):
    n = p.shape[0]
    blk = lambda c: pl.BlockSpec((T, DIFF_WIDTH), lambda i: (i, c))
    tab = pl.BlockSpec((T, DIFF_QK_DIM), lambda i: (i, 0))
    shp = jax.ShapeDtypeStruct((n, DIFF_WIDTH), BF16)
    return pl.pallas_call(
        _rope_kernel,
        grid=(n // T,),
        in_specs=[blk(0), blk(1), blk(2), tab, tab],
        out_specs=[blk(0), blk(0), blk(0)],
        out_shape=[shp, shp, shp],
        compiler_params=_cparams(("arbitrary",)),
        name="rope_prep",
    )(p, p, p, cos, sin)


def _attn_kernel(q_ref, k_ref, v_ref, dl_ref, sub_ref, o_ref, acc_ref, m_ref, l_ref,
                 *, nlat_q, n_lat_keys, n_ctx_keys, lam_init):
    qi = pl.program_id(1)
    m_ref[...] = jnp.full(m_ref.shape, -jnp.inf, F32)
    l_ref[...] = jnp.zeros(l_ref.shape, F32)
    acc_ref[...] = jnp.zeros(acc_ref.shape, F32)

    def chunk(start, size):
        vv = v_ref[pl.ds(start, size), :]
        for c in range(2):
            cols = slice(c * DIFF_QK_DIM, (c + 1) * DIFF_QK_DIM)
            s = _dot_nt(q_ref[:, cols], k_ref[pl.ds(start, size), cols])
            m_old = m_ref[c]
            m_new = jnp.maximum(m_old, jnp.max(s, axis=-1, keepdims=True))
            alpha = jnp.exp(m_old - m_new)
            p = jnp.exp(s - m_new)
            l_ref[c] = alpha * l_ref[c] + jnp.sum(p, axis=-1, keepdims=True)
            acc_ref[c] = alpha * acc_ref[c] + _dot(p.astype(BF16), vv)
            m_ref[c] = m_new

    def lat_body(j, carry):
        chunk(pl.multiple_of(j * ATT_TK, ATT_TK), ATT_TK)
        return carry

    n_chunks = jnp.where(qi < nlat_q, n_lat_keys // ATT_TK, 0)
    lax.fori_loop(0, n_chunks, lat_body, 0)
    chunk(n_lat_keys, n_ctx_keys)

    dl = dl_ref[...]
    lam = (jnp.exp(jnp.sum(dl[0:1] * dl[1:2], axis=-1, keepdims=True))
           - jnp.exp(jnp.sum(dl[2:3] * dl[3:4], axis=-1, keepdims=True)) + lam_init)
    o = acc_ref[0] / l_ref[0] - lam * (acc_ref[1] / l_ref[1])
    o = o * lax.rsqrt(jnp.mean(o * o, axis=-1, keepdims=True) + RMS_EPS) * sub_ref[...]
    o_ref[...] = (o * (1.0 - lam_init)).astype(BF16)


def _diff_attention(q, k, v, dl, subln, nlat, lam_init):
    n = q.shape[0]
    kern = functools.partial(_attn_kernel, nlat_q=nlat, n_lat_keys=nlat * T,
                             n_ctx_keys=n - nlat * T, lam_init=lam_init)
    return pl.pallas_call(
        kern,
        grid=(DIFF_HEADS, n // T),
        in_specs=[
            pl.BlockSpec((T, DIFF_V_DIM), lambda h, i: (i, h)),
            pl.BlockSpec((n, DIFF_V_DIM), lambda h, i: (0, h)),
            pl.BlockSpec((n, DIFF_V_DIM), lambda h, i: (0, h)),
            pl.BlockSpec((4, DIFF_QK_DIM), lambda h, i: (0, 0)),
            pl.BlockSpec((1, DIFF_V_DIM), lambda h, i: (0, 0)),
        ],
        out_specs=pl.BlockSpec((T, DIFF_V_DIM), lambda h, i: (i, h)),
        out_shape=jax.ShapeDtypeStruct((n, DIFF_WIDTH), BF16),
        scratch_shapes=[pltpu.VMEM((2, T, DIFF_V_DIM), F32), pltpu.VMEM((2, T, 1), F32),
                        pltpu.VMEM((2, T, 1), F32)],
        compiler_params=_cparams(("arbitrary", "arbitrary")),
        name="diff_attention",
    )(q, k, v, dl, subln)


def _odd_out_kernel(on_ref, xin_ref, gb_ref, gc_ref, xinp_ref, gcp_ref, xinn_ref, gcn_ref,
                    cw_ref, x_ref, m_ref, wout_ref, g_ref, b_ref, o_ref, e_ref, cat_ref,
                    *, nlat, ntiles):
    i = pl.program_id(0)
    first, last, _, _ = _seq_flags(i, nlat, ntiles)
    u = gc_ref[...] * xin_ref[...]
    e_ref[0:HALO, :] = jnp.where(first, 0.0, gcp_ref[...] * xinp_ref[...])
    e_ref[HALO:HALO + T, :] = u
    e_ref[HALO + T:HALO + T + HALO, :] = jnp.where(last, 0.0, gcn_ref[...] * xinn_ref[...])
    cw = cw_ref[...]
    z = (cw[0:1, :] * e_ref[HALO - 1:HALO - 1 + T, :] + cw[1:2, :] * u
         + cw[2:3, :] * e_ref[HALO + 1:HALO + 1 + T, :])
    cat_ref[:, 0:DIFF_WIDTH] = on_ref[...]
    cat_ref[:, DIFF_WIDTH:D] = (gb_ref[...] * z).astype(BF16)
    y = _dot(cat_ref[...], wout_ref[...])
    o_ref[...] = _residual_ln(x_ref[...], y, m_ref[0], 2, g_ref[...], b_ref[...])


def _odd_out(on, p, conv_w, x, mod, wout_bf, g, b, nlat):
    n = x.shape[0]
    ntiles = n // T
    hb = T // HALO
    nhb = n // HALO
    kern = functools.partial(_odd_out_kernel, nlat=nlat, ntiles=ntiles)
    const2 = lambda i: (0, 0)
    blk = lambda c: pl.BlockSpec((T, CONV_WIDTH), lambda i: (i, c))
    prev = lambda c: pl.BlockSpec((HALO, CONV_WIDTH), lambda i: (jnp.maximum(i * hb - 1, 0), c))
    nxt = lambda c: pl.BlockSpec((HALO, CONV_WIDTH), lambda i: (jnp.minimum((i + 1) * hb, nhb - 1), c))
    return pl.pallas_call(
        kern,
        grid=(ntiles,),
        in_specs=[
            pl.BlockSpec((T, DIFF_WIDTH), lambda i: (i, 0)),
            blk(3), blk(4), blk(5), prev(3), prev(5), nxt(3), nxt(5),
            pl.BlockSpec((3, CONV_WIDTH), const2),
            pl.BlockSpec((T, D), lambda i: (i, 0)),
            pl.BlockSpec((1, MOD_ROWS, D), lambda i: (i // nlat, 0, 0)),
            pl.BlockSpec((D, D), const2),
            pl.BlockSpec((1, D), const2),
            pl.BlockSpec((1, D), const2),
        ],
        out_specs=pl.BlockSpec((T, D), lambda i: (i, 0)),
        out_shape=jax.ShapeDtypeStruct((n, D), F32),
        scratch_shapes=[pltpu.VMEM((T + 2 * HALO, CONV_WIDTH), F32), pltpu.VMEM((T, D), BF16)],
        compiler_params=_cparams(("arbitrary",)),
        name="odd_out",
    )(on, p, p, p, p, p, p, p, conv_w, x, mod, wout_bf, g, b)


def _router_kernel(x_ref, m_ref, rw_ref, rb_ref, eidx_ref, gate_ref, rank_ref, cnt_ref, base_ref):
    i = pl.program_id(0)

    @pl.when(i == 0)
    def _():
        base_ref[...] = jnp.zeros(base_ref.shape, F32)

    m = m_ref[0]
    h = x_ref[...] * (1.0 + m[4:5, :]) + m[3:4, :]
    h_hi = h.astype(BF16)
    h_lo = (h - h_hi.astype(F32)).astype(BF16)
    rw = rw_ref[...]
    rw_hi = rw.astype(BF16)
    rw_lo = (rw - rw_hi.astype(F32)).astype(BF16)
    logits = _dot_nt(rw_hi, h_hi) + (_dot_nt(rw_hi, h_lo) + _dot_nt(rw_lo, h_hi))
    scores = _sigmoid(logits)
    biased = scores + rb_ref[...]

    neg = -jnp.inf
    b3 = biased.reshape(N_GROUPS, GROUP_SIZE, T)
    io3 = lax.broadcasted_iota(I32, b3.shape, 1)
    m1 = jnp.max(b3, axis=1, keepdims=True)
    f1 = jnp.min(jnp.where(b3 == m1, io3, GROUP_SIZE), axis=1, keepdims=True)
    m2 = jnp.max(jnp.where(io3 == f1, neg, b3), axis=1, keepdims=True)
    gs = (m1 + m2).reshape(N_GROUPS, T)

    gio = lax.broadcasted_iota(I32, gs.shape, 0)
    gsel = jnp.zeros(gs.shape, F32)
    for _ in range(TOPK_GROUPS):
        mx = jnp.max(gs, axis=0, keepdims=True)
        f = jnp.min(jnp.where(gs == mx, gio, N_GROUPS), axis=0, keepdims=True)
        hit = gio == f
        gsel = jnp.where(hit, 1.0, gsel)
        gs = jnp.where(hit, neg, gs)
    masked = jnp.where(gsel.reshape(N_GROUPS, 1, T) > 0.5, b3, neg).reshape(N_EXPERTS, T)

    eio = lax.broadcasted_iota(I32, masked.shape, 0)
    hits, gates, eids = [], [], []
    onehot = jnp.zeros(masked.shape, F32)
    for _ in range(TOP_K):
        mx = jnp.max(masked, axis=0, keepdims=True)
        f = jnp.min(jnp.where(masked == mx, eio, N_EXPERTS), axis=0, keepdims=True)
        hit = eio == f
        hits.append(hit)
        eids.append(f)
        gates.append(jnp.sum(jnp.where(hit, scores, 0.0), axis=0, keepdims=True))
        onehot = jnp.where(hit, 1.0, onehot)
        masked = jnp.where(hit, neg, masked)
    gsum = gates[0]
    for gk in gates[1:]:
        gsum = gsum + gk

    r_io = lax.broadcasted_iota(I32, (T, T), 0)
    c_io = lax.broadcasted_iota(I32, (T, T), 1)
    upper = jnp.where(r_io < c_io, 1.0, 0.0).astype(BF16)
    before = base_ref[...] + _dot(onehot.astype(BF16), upper)
    for k in range(TOP_K):
        eidx_ref[k:k + 1, :] = eids[k]
        gate_ref[k:k + 1, :] = gates[k] / gsum * ROUTED_SCALE
        rank_ref[k:k + 1, :] = jnp.sum(jnp.where(hits[k], before, 0.0), axis=0, keepdims=True).astype(I32)
    base_new = base_ref[...] + jnp.sum(onehot, axis=1, keepdims=True)
    base_ref[...] = base_new
    cnt_ref[...] = jnp.broadcast_to(base_new, cnt_ref.shape)


def _router(x, mod, rw_t, rb_col, nlat):
    n = x.shape[0]
    kt = pl.BlockSpec((TOP_K, T), lambda i: (0, i))
    return pl.pallas_call(
        _router_kernel,
        grid=(n // T,),
        in_specs=[
            pl.BlockSpec((T, D), lambda i: (i, 0)),
            pl.BlockSpec((1, MOD_ROWS, D), lambda i: (i // nlat, 0, 0)),
            pl.BlockSpec((N_EXPERTS, D), lambda i: (0, 0)),
            pl.BlockSpec((N_EXPERTS, 1), lambda i: (0, 0)),
        ],
        out_specs=[kt, kt, kt, pl.BlockSpec((N_EXPERTS, 128), lambda i: (0, 0))],
        out_shape=[jax.ShapeDtypeStruct((TOP_K, n), I32), jax.ShapeDtypeStruct((TOP_K, n), F32),
                   jax.ShapeDtypeStruct((TOP_K, n), I32), jax.ShapeDtypeStruct((N_EXPERTS, 128), F32)],
        scratch_shapes=[pltpu.VMEM((N_EXPERTS, 1), F32)],
        compiler_params=_cparams(("arbitrary",)),
        name="moe_router",
    )(x, mod, rw_t, rb_col)


def _row_copy(src, src_row, dst, dst_row, sem):
    return pltpu.make_async_copy(src.at[pl.ds(src_row, 1), :], dst.at[pl.ds(dst_row, 1), :], sem)


def _dispatch_kernel(dest_ref, x_ref, m_ref, xs_ref, h_ref, sem):
    m = m_ref[0]
    h_ref[...] = x_ref[...] * (1.0 + m[4:5, :]) + m[3:4, :]

    def body(t, carry):
        for k in range(TOP_K):
            _row_copy(h_ref, t, xs_ref, dest_ref[0, k, t], sem).start()
        return carry

    lax.fori_loop(0, T, body, 0)
    for _ in range(TOP_K):
        pltpu.make_async_copy(h_ref, xs_ref.at[pl.ds(0, T), :], sem).wait()


def _dispatch(dest3, x, mod, nlat):
    n = x.shape[0]
    return pl.pallas_call(
        _dispatch_kernel,
        grid=(n // T,),
        in_specs=[
            pl.BlockSpec((1, TOP_K, T), lambda i: (i, 0, 0), memory_space=pltpu.SMEM),
            pl.BlockSpec((T, D), lambda i: (i, 0)),
            pl.BlockSpec((1, MOD_ROWS, D), lambda i: (i // nlat, 0, 0)),
        ],
        out_specs=pl.BlockSpec(memory_space=pl.ANY),
        out_shape=jax.ShapeDtypeStruct((n * TOP_K, D), F32),
        scratch_shapes=[pltpu.VMEM((T, D), F32), pltpu.SemaphoreType.DMA],
        compiler_params=_cparams(("arbitrary",)),
        name="moe_dispatch",
    )(dest3, x, mod)


def _expert_kernel(vt_ref, ve_ref, vlo_ref, vhi_ref, nv_ref, xs_ref, wg_ref, wu_ref, wd_ref, ys_ref):
    v = pl.program_id(0)

    @pl.when(v < nv_ref[0])
    def _():
        xb = xs_ref[...].astype(BF16)
        h1 = _dot(xb, wg_ref[0].astype(BF16))
        h2 = _dot(xb, wu_ref[0].astype(BF16))
        y = _dot((_silu(h1) * h2).astype(BF16), wd_ref[0].astype(BF16))
        row = lax.broadcasted_iota(I32, (T, 1), 0)
        lo = vlo_ref[v]
        mine = jnp.logical_and(row >= lo, row < vhi_ref[v])

        @pl.when(lo == 0)
        def _():
            ys_ref[...] = jnp.where(mine, y, 0.0)

        @pl.when(lo != 0)
        def _():
            ys_ref[...] = jnp.where(mine, y, ys_ref[...])


def _expert_ffn(sched, xs, w_gate, w_up, w_down):
    vt, ve, vlo, vhi, nv = sched
    nslots = xs.shape[0]
    nvis = vt.shape[0]
    grid_spec = pltpu.PrefetchScalarGridSpec(
        num_scalar_prefetch=5,
        grid=(nvis,),
        in_specs=[
            pl.BlockSpec((T, D), lambda v, vt, ve, vlo, vhi, nv: (vt[v], 0)),
            pl.BlockSpec((1, D, EXPERT_FF), lambda v, vt, ve, vlo, vhi, nv: (ve[v], 0, 0)),
            pl.BlockSpec((1, D, EXPERT_FF), lambda v, vt, ve, vlo, vhi, nv: (ve[v], 0, 0)),
            pl.BlockSpec((1, EXPERT_FF, D), lambda v, vt, ve, vlo, vhi, nv: (ve[v], 0, 0)),
        ],
        out_specs=pl.BlockSpec((T, D), lambda v, vt, ve, vlo, vhi, nv: (vt[v], 0)),
    )
    return pl.pallas_call(
        _expert_kernel,
        grid_spec=grid_spec,
        out_shape=jax.ShapeDtypeStruct((nslots, D), F32),
        compiler_params=_cparams(("arbitrary",)),
        name="moe_experts",
    )(vt, ve, vlo, vhi, nv, xs, w_gate, w_up, w_down)


def _combine_kernel(dest_ref, ys_ref, x_ref, m_ref, gt_ref, sg_ref, su_ref, sd_ref, g_ref, b_ref,
                    o_ref, buf_ref, sem):
    def body(t, carry):
        for k in range(TOP_K):
            pltpu.make_async_copy(ys_ref.at[pl.ds(dest_ref[0, k, t], 1), :],
                                  buf_ref.at[k, pl.ds(t, 1), :], sem).start()
        return carry

    lax.fori_loop(0, T, body, 0)

    m = m_ref[0]
    x = x_ref[...]
    hb = (x * (1.0 + m[4:5, :]) + m[3:4, :]).astype(BF16)
    f = _dot((_silu(_dot(hb, sg_ref[...])) * _dot(hb, su_ref[...])).astype(BF16), sd_ref[...])

    for k in range(TOP_K):
        pltpu.make_async_copy(ys_ref.at[pl.ds(0, T), :], buf_ref.at[k], sem).wait()
    gt = gt_ref[...]
    for k in range(TOP_K):
        f = f + gt[:, k:k + 1] * buf_ref[k]
    o_ref[...] = _residual_ln(x, f, m, 5, g_ref[...], b_ref[...])


def _combine(dest3, ys, x, mod, gates_t, sg_bf, su_bf, sd_bf, g, b, nlat):
    n = x.shape[0]
    const2 = lambda i: (0, 0)
    return pl.pallas_call(
        _combine_kernel,
        grid=(n // T,),
        in_specs=[
            pl.BlockSpec((1, TOP_K, T), lambda i: (i, 0, 0), memory_space=pltpu.SMEM),
            pl.BlockSpec(memory_space=pl.ANY),
            pl.BlockSpec((T, D), lambda i: (i, 0)),
            pl.BlockSpec((1, MOD_ROWS, D), lambda i: (i // nlat, 0, 0)),
            pl.BlockSpec((T, TOP_K), lambda i: (i, 0)),
            pl.BlockSpec((D, EXPERT_FF), const2),
            pl.BlockSpec((D, EXPERT_FF), const2),
            pl.BlockSpec((EXPERT_FF, D), const2),
            pl.BlockSpec((1, D), const2),
            pl.BlockSpec((1, D), const2),
        ],
        out_specs=pl.BlockSpec((T, D), lambda i: (i, 0)),
        out_shape=jax.ShapeDtypeStruct((n, D), F32),
        scratch_shapes=[pltpu.VMEM((TOP_K, T, D), F32), pltpu.SemaphoreType.DMA],
        compiler_params=_cparams(("arbitrary",)),
        name="moe_combine",
    )(dest3, ys, x, mod, gates_t, sg_bf, su_bf, sd_bf, g, b)


def _slot_schedule(counts, n_slot_tiles):
    offs = jnp.cumsum(counts) - counts
    ends = offs + counts
    first_tile = offs // T
    last_tile = jnp.maximum(ends - 1, offs) // T
    nvis_e = jnp.where(counts > 0, last_tile - first_tile + 1, 0)
    vstart = jnp.cumsum(nvis_e) - nvis_e
    total = jnp.sum(nvis_e)
    nvis = n_slot_tiles + N_EXPERTS - 1
    v = jnp.arange(nvis, dtype=I32)
    e = jnp.clip(jnp.searchsorted(vstart + nvis_e, v, side="right"), 0, N_EXPERTS - 1).astype(I32)
    tile = first_tile[e] + (v - vstart[e])
    lo = jnp.maximum(offs[e], tile * T) - tile * T
    hi = jnp.minimum(ends[e], (tile + 1) * T) - tile * T
    valid = v < total
    e_last = e[jnp.maximum(total - 1, 0)]
    vt = jnp.where(valid, tile, n_slot_tiles - 1).astype(I32)
    ve = jnp.where(valid, e, e_last).astype(I32)
    vlo = jnp.where(valid, lo, 0).astype(I32)
    vhi = jnp.where(valid, hi, 0).astype(I32)
    return offs, (vt, ve, vlo, vhi, total.astype(I32).reshape(1))


def _moe_layer(x, mod, rw_t, rb_col, w_gate, w_up, w_down, sg_bf, su_bf, sd_bf, g, b, nlat):
    n = x.shape[0]
    eidx, gates, rank, cnt = _router(x, mod, rw_t, rb_col, nlat)
    counts = cnt[:, 0].astype(I32)
    offs, sched = _slot_schedule(counts, n * TOP_K // T)
    dest = offs[eidx] + rank
    dest3 = dest.reshape(TOP_K, n // T, T).transpose(1, 0, 2)
    xs = _dispatch(dest3, x, mod, nlat)
    ys = _expert_ffn(sched, xs, w_gate, w_up, w_down)
    return _combine(dest3, ys, x, mod, gates.T, sg_bf, su_bf, sd_bf, g, b, nlat)


def _rope_tables(n_lat, n_ctx):
    rows = n_lat // GRID_W
    r, cidx = jnp.meshgrid(jnp.arange(rows), jnp.arange(GRID_W), indexing="ij")
    pos = jnp.stack([r.reshape(-1), cidx.reshape(-1)], axis=-1).astype(F32)
    nf = DIFF_QK_DIM // 4
    inv = ROPE_BASE ** (-jnp.arange(nf, dtype=F32) / nf)
    ang = jnp.broadcast_to(pos[:, :, None, None] * inv, (n_lat, 2, 2, nf)).reshape(n_lat, DIFF_QK_DIM)
    cos = jnp.concatenate([jnp.cos(ang), jnp.ones((n_ctx, DIFF_QK_DIM), F32)], axis=0)
    sin = jnp.concatenate([jnp.sin(ang), jnp.zeros((n_ctx, DIFF_QK_DIM), F32)], axis=0)
    return cos, sin


def kernel(x, c, ctx, c_ctx, ada_w, ada_b, ln_g, ln_b, ev_w_in, ev_w_out, pool_w, pool_scale, sgu_ln_g, sgu_ln_b, sgu_w, sgu_b, od_w_in, od_w_out, diff_lambda, diff_subln, conv_w, router_w, router_bias, exp_w_gate, exp_w_up, exp_w_down, sh_w_gate, sh_w_up, sh_w_down):
    bsz, n_lat, d = x.shape
    n_ctx = ctx.shape[1]
    assert bsz == 1 and d == D and n_lat % T == 0 and n_ctx % T == 0 and n_lat % ATT_TK == 0
    depth = ada_w.shape[0]
    nlat = n_lat // T

    xs = jnp.concatenate([x[0], ctx[0]], axis=0)
    mods = _ada_mods(jnp.stack([c[0], c_ctx], axis=1), ada_w, ada_b)
    cos, sin = _rope_tables(n_lat, n_ctx)
    eye = jnp.eye(T // SGU_CHUNK, dtype=F32)

    for li in range(depth):
        j = li // 2
        mod = mods[li]
        g0, b0 = ln_g[li, 0][None], ln_b[li, 0][None]
        g1, b1 = ln_g[li, 1][None], ln_b[li, 1][None]
        if li % 2 == 0:
            p = _mod_matmul(xs, mod, ev_w_in[j].astype(BF16), nlat, sh=0, sc=1, tn=EVEN_IN)
            sguw_bd = jnp.einsum("ab,hpq->hapbq", eye, sgu_w[j]).reshape(SGU_HEADS, T, T).astype(BF16)
            sgub_t = jnp.tile(sgu_b[j].T, (T // SGU_CHUNK, 1))
            xs = _even_mix(p, xs, mod, pool_w[j].astype(BF16), pool_scale[j], sgu_ln_g[j][None], sgu_ln_b[j][None],
                           sguw_bd, sgub_t, ev_w_out[j].astype(BF16), g0, b0, nlat)
        else:
            p = _mod_matmul(xs, mod, od_w_in[j].astype(BF16), nlat, sh=0, sc=1, tn=ODD_IN // 2)
            q, k, v = _rope_prep(p, cos, sin)
            lam_init = 0.8 - 0.6 * math.exp(-0.3 * li)
            on = _diff_attention(q, k, v, diff_lambda[j], diff_subln[j][None], nlat, lam_init)
            xs = _odd_out(on, p, conv_w[j], xs, mod, od_w_out[j].astype(BF16), g0, b0, nlat)
        xs = _moe_layer(xs, mod, router_w[li].T, router_bias[li][:, None], exp_w_gate[li], exp_w_up[li],
                        exp_w_down[li], sh_w_gate[li].astype(BF16), sh_w_up[li].astype(BF16),
                        sh_w_down[li].astype(BF16), g1, b1, nlat)
    return xs[:n_lat][None]
```

```python
import functools
import math

import jax
import jax.numpy as jnp
from jax import lax
from jax.experimental import pallas as pl
from jax.experimental.pallas import tpu as pltpu

F32 = jnp.float32
BF16 = jnp.bfloat16
I32 = jnp.int32

D = 2048
DEPTH = 4
GRID_W = 64

POOL_GROUPS = 4
POOL_WINDOWS = (2, 4, 8, 16)
POOL_WIDTH = 1024
POOL_GROUP_DIM = 256
SGU_WIDTH = 1024
SGU_CHUNK = 128
SGU_HEADS = 8
EVEN_IN = 3072

DIFF_HEADS = 4
DIFF_QK_DIM = 128
DIFF_V_DIM = 256
DIFF_WIDTH = 1024
CONV_WIDTH = 1024
ODD_IN = 6144
ROPE_BASE = 10000.0

N_EXPERTS = 64
EXPERT_FF = 384
TOP_K = 8
N_GROUPS = 8
GROUP_SIZE = 8
TOPK_GROUPS = 4
ROUTED_SCALE = 2.5

ALPHA = (2 * DEPTH) ** 0.25
LN_EPS = 1e-6
RMS_EPS = 1e-5

T = 256
HALO = 8
MOD_ROWS = 8
ADA_TN = 1536
ADA_UNROLL = 8
ATT_TQ = 512
ATT_TK = 512
ATT_RB = 64
LANES = 128
LOG2E = 1.4426950408889634
VMEM_LIMIT = 56 * 1024 * 1024


def _cparams(sem, vmem=VMEM_LIMIT):
    return pltpu.CompilerParams(dimension_semantics=sem, vmem_limit_bytes=vmem)


def _sigmoid(x):
    return 1.0 / (1.0 + jnp.exp(-x))


def _silu(x):
    return x * _sigmoid(x)


def _layer_norm(v, g, b):
    mu = jnp.mean(v, axis=-1, keepdims=True)
    d = v - mu
    var = jnp.mean(d * d, axis=-1, keepdims=True)
    return d * lax.rsqrt(var + LN_EPS) * g + b


def _dot(a, b):
    return jnp.dot(a, b, preferred_element_type=F32)


def _dot_nt(a, b):
    return lax.dot_general(a, b, (((1,), (1,)), ((), ())), preferred_element_type=F32)


def _ada_kernel(c_ref, w_ref, b_ref, o_ref, s_ref):
    tn = w_ref.shape[2]
    s_ref[...] = _silu(c_ref[...])

    def body(m, carry):
        a0, a1 = carry
        for u in range(ADA_UNROLL):
            r = pl.multiple_of(m * (8 * ADA_UNROLL) + 8 * u, 8)
            w = w_ref[0, pl.ds(r, 8), :]
            s = s_ref[pl.ds(r, 8), :]
            a0 = a0 + w * s[:, 0:1]
            a1 = a1 + w * s[:, 1:2]
        return a0, a1

    z = jnp.zeros((8, tn), F32)
    a0, a1 = lax.fori_loop(0, D // (8 * ADA_UNROLL), body, (z, z))
    bias = b_ref[0]
    o_ref[0] = jnp.zeros((8, tn), F32)
    o_ref[0, 0:1, :] = jnp.sum(a0, axis=0, keepdims=True) + bias
    o_ref[0, 1:2, :] = jnp.sum(a1, axis=0, keepdims=True) + bias


def _ada_mods(c_cols, ada_w, ada_b):
    depth = ada_w.shape[0]
    n6 = ada_w.shape[2]
    out = pl.pallas_call(
        _ada_kernel,
        grid=(depth, n6 // ADA_TN),
        in_specs=[
            pl.BlockSpec((D, 2), lambda l, j: (0, 0)),
            pl.BlockSpec((1, D, ADA_TN), lambda l, j: (l, 0, j)),
            pl.BlockSpec((1, 1, ADA_TN), lambda l, j: (l, 0, j)),
        ],
        out_specs=pl.BlockSpec((1, 8, ADA_TN), lambda l, j: (l, 0, j)),
        out_shape=jax.ShapeDtypeStruct((depth, 8, n6), F32),
        scratch_shapes=[pltpu.VMEM((D, 2), F32)],
        compiler_params=_cparams(("arbitrary", "arbitrary")),
        name="ada_mods",
    )(c_cols, ada_w, ada_b.reshape(depth, 1, n6))
    mods = out[:, :2, :].reshape(depth, 2, 6, D)
    return jnp.pad(mods, ((0, 0), (0, 0), (0, MOD_ROWS - 6), (0, 0)))


def _modmm_kernel(x_ref, m_ref, w_ref, o_ref, *, sh, sc):
    m = m_ref[0]
    h = x_ref[...] * (1.0 + m[sc:sc + 1, :]) + m[sh:sh + 1, :]
    o_ref[...] = _dot(h.astype(BF16), w_ref[...])


def _mod_matmul(x, mod, w_bf, nlat, *, sh, sc, tn):
    n = x.shape[0]
    nout = w_bf.shape[1]
    return pl.pallas_call(
        functools.partial(_modmm_kernel, sh=sh, sc=sc),
        grid=(nout // tn, n // T),
        in_specs=[
            pl.BlockSpec((T, D), lambda j, i: (i, 0)),
            pl.BlockSpec((1, MOD_ROWS, D), lambda j, i: (i // nlat, 0, 0)),
            pl.BlockSpec((D, tn), lambda j, i: (0, j)),
        ],
        out_specs=pl.BlockSpec((T, tn), lambda j, i: (i, j)),
        out_shape=jax.ShapeDtypeStruct((n, nout), F32),
        compiler_params=_cparams(("arbitrary", "arbitrary")),
        name="mod_matmul",
    )(x, mod, w_bf)


def _seq_flags(i, nlat, ntiles):
    is_ctx = i >= nlat
    first = jnp.logical_or(i == 0, i == nlat)
    last = jnp.logical_or(i == nlat - 1, i == ntiles - 1)
    t0 = jnp.where(is_ctx, i - nlat, i) * T
    nseq = jnp.where(is_ctx, (ntiles - nlat) * T, nlat * T)
    return first, last, t0, nseq


def _residual_ln(x, y, m, gate_row, g, b):
    return _layer_norm(ALPHA * x + m[gate_row:gate_row + 1, :] * y, g, b)


def _even_mix_kernel(pp_ref, pprev_ref, pnext_ref, pu_ref, pv_ref, x_ref, m_ref,
                     poolw_ref, pscale_ref, slng_ref, slnb_ref, sguw_ref, sgub_ref,
                     wout_ref, g_ref, b_ref, o_ref, e_ref, cat_ref, *, nlat, ntiles):
    i = pl.program_id(0)
    first, last, t0, nseq = _seq_flags(i, nlat, ntiles)

    e_ref[0:HALO, :] = jnp.where(first, 0.0, pprev_ref[...])
    e_ref[HALO:HALO + T, :] = pp_ref[...]
    e_ref[HALO + T:HALO + T + HALO, :] = jnp.where(last, 0.0, pnext_ref[...])

    pos = t0 + lax.broadcasted_iota(I32, (T, 1), 0)
    for g in range(POOL_GROUPS):
        w = POOL_WINDOWS[g]
        lo, hi = w // 2, w - 1 - w // 2
        cols = slice(g * POOL_GROUP_DIM, (g + 1) * POOL_GROUP_DIM)
        tot = e_ref[HALO - lo:HALO - lo + T, cols]
        for d in range(-lo + 1, hi + 1):
            tot = tot + e_ref[HALO + d:HALO + d + T, cols]
        cnt = jnp.minimum(pos + hi, nseq - 1) - jnp.maximum(pos - lo, 0) + 1
        pooled = tot / cnt.astype(F32) - e_ref[HALO:HALO + T, cols]
        a = _dot(pooled.astype(BF16), poolw_ref[g]) * pscale_ref[g:g + 1, :]
        cat_ref[:, cols] = a.astype(BF16)

    zu = jax.nn.gelu(pu_ref[...])
    v = _layer_norm(jax.nn.gelu(pv_ref[...]), slng_ref[...], slnb_ref[...]).astype(BF16)
    for h in range(SGU_HEADS):
        cols = slice(h * SGU_CHUNK, (h + 1) * SGU_CHUNK)
        mixed = _dot(sguw_ref[h], v[:, cols]) + sgub_ref[:, h:h + 1]
        cat_ref[:, POOL_WIDTH + h * SGU_CHUNK:POOL_WIDTH + (h + 1) * SGU_CHUNK] = (zu[:, cols] * mixed).astype(BF16)

    y = _dot(cat_ref[...], wout_ref[...])
    o_ref[...] = _residual_ln(x_ref[...], y, m_ref[0], 2, g_ref[...], b_ref[...])


def _even_mix(p, x, mod, poolw_bf, pscale, slng, slnb, sguw_bd, sgub_t, wout_bf, g, b, nlat):
    n = x.shape[0]
    ntiles = n // T
    hb = T // HALO
    nhb = n // HALO
    kern = functools.partial(_even_mix_kernel, nlat=nlat, ntiles=ntiles)
    const2 = lambda i: (0, 0)
    const3 = lambda i: (0, 0, 0)
    return pl.pallas_call(
        kern,
        grid=(ntiles,),
        in_specs=[
            pl.BlockSpec((T, POOL_WIDTH), lambda i: (i, 0)),
            pl.BlockSpec((HALO, POOL_WIDTH), lambda i: (jnp.maximum(i * hb - 1, 0), 0)),
            pl.BlockSpec((HALO, POOL_WIDTH), lambda i: (jnp.minimum((i + 1) * hb, nhb - 1), 0)),
            pl.BlockSpec((T, SGU_WIDTH), lambda i: (i, 1)),
            pl.BlockSpec((T, SGU_WIDTH), lambda i: (i, 2)),
            pl.BlockSpec((T, D), lambda i: (i, 0)),
            pl.BlockSpec((1, MOD_ROWS, D), lambda i: (i // nlat, 0, 0)),
            pl.BlockSpec((POOL_GROUPS, POOL_GROUP_DIM, POOL_GROUP_DIM), const3),
            pl.BlockSpec((POOL_GROUPS, POOL_GROUP_DIM), const2),
            pl.BlockSpec((1, SGU_WIDTH), const2),
            pl.BlockSpec((1, SGU_WIDTH), const2),
            pl.BlockSpec((SGU_HEADS, T, T), const3),
            pl.BlockSpec((T, SGU_HEADS), const2),
            pl.BlockSpec((D, D), const2),
            pl.BlockSpec((1, D), const2),
            pl.BlockSpec((1, D), const2),
        ],
        out_specs=pl.BlockSpec((T, D), lambda i: (i, 0)),
        out_shape=jax.ShapeDtypeStruct((n, D), F32),
        scratch_shapes=[pltpu.VMEM((T + 2 * HALO, POOL_WIDTH), F32), pltpu.VMEM((T, D), BF16)],
        compiler_params=_cparams(("arbitrary",)),
        name="even_mix",
    )(p, p, p, p, p, x, mod, poolw_bf, pscale, slng, slnb, sguw_bd, sgub_t, wout_bf, g, b)


def _rope_kernel(pq_ref, pk_ref, pv_ref, cos_ref, sin_ref, q_ref, k_ref, v_ref):
    cos = cos_ref[...]
    sin = sin_ref[...]
    lane = lax.broadcasted_iota(I32, (1, DIFF_QK_DIM), 1)
    low_half = (lane % 64) < 32
    q_scale = DIFF_QK_DIM ** -0.5 * LOG2E

    def rope(x):
        rot = jnp.where(low_half, -pltpu.roll(x, 96, 1), pltpu.roll(x, 32, 1))
        return x * cos + rot * sin

    for j in range(DIFF_WIDTH // DIFF_QK_DIM):
        cols = slice(j * DIFF_QK_DIM, (j + 1) * DIFF_QK_DIM)
        q_ref[:, cols] = (rope(pq_ref[:, cols]) * q_scale).astype(BF16)
        k_ref[:, cols] = rope(pk_ref[:, cols]).astype(BF16)
    v_ref[...] = pv_ref[...].astype(BF16)


def _rope_prep(p, cos, sin):
    n = p.shape[0]
    blk = lambda c: pl.BlockSpec((T, DIFF_WIDTH), lambda i: (i, c))
    tab = pl.BlockSpec((T, DIFF_QK_DIM), lambda i: (i, 0))
    shp = jax.ShapeDtypeStruct((n, DIFF_WIDTH), BF16)
    return pl.pallas_call(
        _rope_kernel,
        grid=(n // T,),
        in_specs=[blk(0), blk(1), blk(2), tab, tab],
        out_specs=[blk(0), blk(0), blk(0)],
        out_shape=[shp, shp, shp],
        compiler_params=_cparams(("arbitrary",)),
        name="rope_prep",
    )(p, p, p, cos, sin)


def _attn_kernel(q_ref, k_ref, v_ref, dl_ref, sub_ref, o_ref, acc_ref, m_ref, l_ref, a_ref, sa_ref, sb_ref,
                 p_ref, *, n_lat_chunks, ctx_start, n_ctx_keys, lam_init):
    m_ref[...] = jnp.full(m_ref.shape, -jnp.inf, F32)
    l_ref[...] = jnp.zeros(l_ref.shape, F32)
    acc_ref[...] = jnp.zeros(acc_ref.shape, F32)

    def scores(start, size, s_ref):
        for c in range(2):
            cols = slice(c * DIFF_QK_DIM, (c + 1) * DIFF_QK_DIM)
            s_ref[c, :, 0:size] = _dot_nt(q_ref[:, cols], k_ref[pl.ds(start, size), cols])

    tq = q_ref.shape[0]

    def softmax_pv(start, size, s_ref):
        vv = v_ref[pl.ds(start, size), :]
        for c in range(2):
            for r in range(tq // ATT_RB):
                rows = slice(r * ATT_RB, (r + 1) * ATT_RB)
                s = s_ref[c, rows, 0:size]
                m_old = m_ref[c, rows, :]
                m_new = jnp.maximum(m_old, jnp.max(s, axis=-1, keepdims=True))
                alpha = jnp.exp2(m_old - m_new)
                lsum = None
                for j in range(size // LANES):
                    pj = jnp.exp2(s[:, j * LANES:(j + 1) * LANES] - m_new)
                    p_ref[c, rows, j * LANES:(j + 1) * LANES] = pj.astype(BF16)
                    lsum = pj if lsum is None else lsum + pj
                l_ref[c, rows, :] = alpha * l_ref[c, rows, :] + lsum
                m_ref[c, rows, :] = m_new
                a_ref[c, rows, :] = alpha
            alpha = a_ref[c]
            pv = _dot(p_ref[c, :, 0:size], vv)
            acc_ref[c] = jnp.concatenate([alpha, alpha], axis=1) * acc_ref[c] + pv

    lat = lambda j: pl.multiple_of(j * ATT_TK, ATT_TK)
    if n_lat_chunks:
        assert n_lat_chunks % 2 == 0
        scores(0, ATT_TK, sa_ref)

        def pair(i, carry):
            j = 2 * i
            scores(lat(j + 1), ATT_TK, sb_ref)
            softmax_pv(lat(j), ATT_TK, sa_ref)
            scores(lat(j + 2), ATT_TK, sa_ref)
            softmax_pv(lat(j + 1), ATT_TK, sb_ref)
            return carry

        lax.fori_loop(0, n_lat_chunks // 2 - 1, pair, 0)
        last = (n_lat_chunks - 2) * ATT_TK
        scores(last + ATT_TK, ATT_TK, sb_ref)
        softmax_pv(last, ATT_TK, sa_ref)
        scores(ctx_start, n_ctx_keys, sa_ref)
        softmax_pv(last + ATT_TK, ATT_TK, sb_ref)
    else:
        scores(ctx_start, n_ctx_keys, sa_ref)
    softmax_pv(ctx_start, n_ctx_keys, sa_ref)

    dl = dl_ref[...]
    lam = (jnp.exp(jnp.sum(dl[0:1] * dl[1:2], axis=-1, keepdims=True))
           - jnp.exp(jnp.sum(dl[2:3] * dl[3:4], axis=-1, keepdims=True)) + lam_init)
    l0 = jnp.sum(l_ref[0], axis=-1, keepdims=True)
    l1 = jnp.sum(l_ref[1], axis=-1, keepdims=True)
    o = acc_ref[0] / l0 - lam * (acc_ref[1] / l1)
    o = o * lax.rsqrt(jnp.mean(o * o, axis=-1, keepdims=True) + RMS_EPS) * sub_ref[...]
    o_ref[...] = (o * (1.0 - lam_init)).astype(BF16)


def _attn_call(q, k, v, dl, subln, lam_init, *, tq, q_tile0, n_q_tiles, n_lat_chunks, ctx_start, n_ctx_keys):
    n = k.shape[0]
    kern = functools.partial(_attn_kernel, n_lat_chunks=n_lat_chunks, ctx_start=ctx_start,
                             n_ctx_keys=n_ctx_keys, lam_init=lam_init)
    resident = lambda: pl.BlockSpec((n, DIFF_V_DIM), lambda h, i: (0, h), pipeline_mode=pl.Buffered(1))
    return pl.pallas_call(
        kern,
        grid=(DIFF_HEADS, n_q_tiles),
        in_specs=[
            pl.BlockSpec((tq, DIFF_V_DIM), lambda h, i: (q_tile0 + i, h)),
            resident(),
            resident(),
            pl.BlockSpec((4, DIFF_QK_DIM), lambda h, i: (0, 0)),
            pl.BlockSpec((1, DIFF_V_DIM), lambda h, i: (0, 0)),
        ],
        out_specs=pl.BlockSpec((tq, DIFF_V_DIM), lambda h, i: (i, h)),
        out_shape=jax.ShapeDtypeStruct((n_q_tiles * tq, DIFF_WIDTH), BF16),
        scratch_shapes=[pltpu.VMEM((2, tq, DIFF_V_DIM), F32), pltpu.VMEM((2, tq, LANES), F32),
                        pltpu.VMEM((2, tq, LANES), F32), pltpu.VMEM((2, tq, LANES), F32),
                        pltpu.VMEM((2, tq, ATT_TK), F32), pltpu.VMEM((2, tq, ATT_TK), F32),
                        pltpu.VMEM((2, tq, ATT_TK), BF16)],
        compiler_params=_cparams(("arbitrary", "arbitrary")),
        name="diff_attention",
    )(q, k, v, dl, subln)


def _diff_attention(q, k, v, dl, subln, n_lat, lam_init):
    n = q.shape[0]
    n_ctx = n - n_lat
    o_lat = _attn_call(q, k, v, dl, subln, lam_init, tq=ATT_TQ, q_tile0=0, n_q_tiles=n_lat // ATT_TQ,
                       n_lat_chunks=n_lat // ATT_TK, ctx_start=n_lat, n_ctx_keys=n_ctx)
    o_ctx = _attn_call(q, k, v, dl, subln, lam_init, tq=n_ctx, q_tile0=n_lat // n_ctx, n_q_tiles=1,
                       n_lat_chunks=0, ctx_start=n_lat, n_ctx_keys=n_ctx)
    return jnp.concatenate([o_lat, o_ctx], axis=0)


def _odd_out_kernel(on_ref, xin_ref, gb_ref, gc_ref, xinp_ref, gcp_ref, xinn_ref, gcn_ref,
                    cw_ref, x_ref, m_ref, wout_ref, g_ref, b_ref, o_ref, e_ref, cat_ref,
                    *, nlat, ntiles):
    i = pl.program_id(0)
    first, last, _, _ = _seq_flags(i, nlat, ntiles)
    u = gc_ref[...] * xin_ref[...]
    e_ref[0:HALO, :] = jnp.where(first, 0.0, gcp_ref[...] * xinp_ref[...])
    e_ref[HALO:HALO + T, :] = u
    e_ref[HALO + T:HALO + T + HALO, :] = jnp.where(last, 0.0, gcn_ref[...] * xinn_ref[...])
    cw = cw_ref[...]
    z = (cw[0:1, :] * e_ref[HALO - 1:HALO - 1 + T, :] + cw[1:2, :] * u
         + cw[2:3, :] * e_ref[HALO + 1:HALO + 1 + T, :])
    cat_ref[:, 0:DIFF_WIDTH] = on_ref[...]
    cat_ref[:, DIFF_WIDTH:D] = (gb_ref[...] * z).astype(BF16)
    y = _dot(cat_ref[...], wout_ref[...])
    o_ref[...] = _residual_ln(x_ref[...], y, m_ref[0], 2, g_ref[...], b_ref[...])


def _odd_out(on, p, conv_w, x, mod, wout_bf, g, b, nlat):
    n = x.shape[0]
    ntiles = n // T
    hb = T // HALO
    nhb = n // HALO
    kern = functools.partial(_odd_out_kernel, nlat=nlat, ntiles=ntiles)
    const2 = lambda i: (0, 0)
    blk = lambda c: pl.BlockSpec((T, CONV_WIDTH), lambda i: (i, c))
    prev = lambda c: pl.BlockSpec((HALO, CONV_WIDTH), lambda i: (jnp.maximum(i * hb - 1, 0), c))
    nxt = lambda c: pl.BlockSpec((HALO, CONV_WIDTH), lambda i: (jnp.minimum((i + 1) * hb, nhb - 1), c))
    return pl.pallas_call(
        kern,
        grid=(ntiles,),
        in_specs=[
            pl.BlockSpec((T, DIFF_WIDTH), lambda i: (i, 0)),
            blk(3), blk(4), blk(5), prev(3), prev(5), nxt(3), nxt(5),
            pl.BlockSpec((3, CONV_WIDTH), const2),
            pl.BlockSpec((T, D), lambda i: (i, 0)),
            pl.BlockSpec((1, MOD_ROWS, D), lambda i: (i // nlat, 0, 0)),
            pl.BlockSpec((D, D), const2),
            pl.BlockSpec((1, D), const2),
            pl.BlockSpec((1, D), const2),
        ],
        out_specs=pl.BlockSpec((T, D), lambda i: (i, 0)),
        out_shape=jax.ShapeDtypeStruct((n, D), F32),
        scratch_shapes=[pltpu.VMEM((T + 2 * HALO, CONV_WIDTH), F32), pltpu.VMEM((T, D), BF16)],
        compiler_params=_cparams(("arbitrary",)),
        name="odd_out",
    )(on, p, p, p, p, p, p, p, conv_w, x, mod, wout_bf, g, b)


def _router_kernel(x_ref, m_ref, rw_ref, rb_ref, eidx_ref, gate_ref, rank_ref, cnt_ref, base_ref):
    i = pl.program_id(0)

    @pl.when(i == 0)
    def _():
        base_ref[...] = jnp.zeros(base_ref.shape, F32)

    m = m_ref[0]
    h = x_ref[...] * (1.0 + m[4:5, :]) + m[3:4, :]
    h_hi = h.astype(BF16)
    h_lo = (h - h_hi.astype(F32)).astype(BF16)
    rw = rw_ref[...]
    rw_hi = rw.astype(BF16)
    rw_lo = (rw - rw_hi.astype(F32)).astype(BF16)
    logits = _dot_nt(rw_hi, h_hi) + (_dot_nt(rw_hi, h_lo) + _dot_nt(rw_lo, h_hi))
    scores = _sigmoid(logits)
    biased = scores + rb_ref[...]

    neg = -jnp.inf
    b3 = biased.reshape(N_GROUPS, GROUP_SIZE, T)
    io3 = lax.broadcasted_iota(I32, b3.shape, 1)
    m1 = jnp.max(b3, axis=1, keepdims=True)
    f1 = jnp.min(jnp.where(b3 == m1, io3, GROUP_SIZE), axis=1, keepdims=True)
    m2 = jnp.max(jnp.where(io3 == f1, neg, b3), axis=1, keepdims=True)
    gs = (m1 + m2).reshape(N_GROUPS, T)

    gio = lax.broadcasted_iota(I32, gs.shape, 0)
    gsel = jnp.zeros(gs.shape, F32)
    for _ in range(TOPK_GROUPS):
        mx = jnp.max(gs, axis=0, keepdims=True)
        f = jnp.min(jnp.where(gs == mx, gio, N_GROUPS), axis=0, keepdims=True)
        hit = gio == f
        gsel = jnp.where(hit, 1.0, gsel)
        gs = jnp.where(hit, neg, gs)
    masked = jnp.where(gsel.reshape(N_GROUPS, 1, T) > 0.5, b3, neg).reshape(N_EXPERTS, T)

    eio = lax.broadcasted_iota(I32, masked.shape, 0)
    hits, gates, eids = [], [], []
    onehot = jnp.zeros(masked.shape, F32)
    for _ in range(TOP_K):
        mx = jnp.max(masked, axis=0, keepdims=True)
        f = jnp.min(jnp.where(masked == mx, eio, N_EXPERTS), axis=0, keepdims=True)
        hit = eio == f
        hits.append(hit)
        eids.append(f)
        gates.append(jnp.sum(jnp.where(hit, scores, 0.0), axis=0, keepdims=True))
        onehot = jnp.where(hit, 1.0, onehot)
        masked = jnp.where(hit, neg, masked)
    gsum = gates[0]
    for gk in gates[1:]:
        gsum = gsum + gk

    r_io = lax.broadcasted_iota(I32, (T, T), 0)
    c_io = lax.broadcasted_iota(I32, (T, T), 1)
    upper = jnp.where(r_io < c_io, 1.0, 0.0).astype(BF16)
    before = base_ref[...] + _dot(onehot.astype(BF16), upper)
    for k in range(TOP_K):
        eidx_ref[k:k + 1, :] = eids[k]
        gate_ref[k:k + 1, :] = gates[k] / gsum * ROUTED_SCALE
        rank_ref[k:k + 1, :] = jnp.sum(jnp.where(hits[k], before, 0.0), axis=0, keepdims=True).astype(I32)
    base_new = base_ref[...] + jnp.sum(onehot, axis=1, keepdims=True)
    base_ref[...] = base_new
    cnt_ref[...] = jnp.broadcast_to(base_new, cnt_ref.shape)


def _router(x, mod, rw_t, rb_col, nlat):
    n = x.shape[0]
    kt = pl.BlockSpec((TOP_K, T), lambda i: (0, i))
    return pl.pallas_call(
        _router_kernel,
        grid=(n // T,),
        in_specs=[
            pl.BlockSpec((T, D), lambda i: (i, 0)),
            pl.BlockSpec((1, MOD_ROWS, D), lambda i: (i // nlat, 0, 0)),
            pl.BlockSpec((N_EXPERTS, D), lambda i: (0, 0)),
            pl.BlockSpec((N_EXPERTS, 1), lambda i: (0, 0)),
        ],
        out_specs=[kt, kt, kt, pl.BlockSpec((N_EXPERTS, 128), lambda i: (0, 0))],
        out_shape=[jax.ShapeDtypeStruct((TOP_K, n), I32), jax.ShapeDtypeStruct((TOP_K, n), F32),
                   jax.ShapeDtypeStruct((TOP_K, n), I32), jax.ShapeDtypeStruct((N_EXPERTS, 128), F32)],
        scratch_shapes=[pltpu.VMEM((N_EXPERTS, 1), F32)],
        compiler_params=_cparams(("arbitrary",)),
        name="moe_router",
    )(x, mod, rw_t, rb_col)


def _row_copy(src, src_row, dst, dst_row, sem):
    return pltpu.make_async_copy(src.at[pl.ds(src_row, 1), :], dst.at[pl.ds(dst_row, 1), :], sem)


def _dispatch_kernel(dest_ref, x_ref, m_ref, xs_ref, h_ref, sem):
    m = m_ref[0]
    h_ref[...] = x_ref[...] * (1.0 + m[4:5, :]) + m[3:4, :]

    def body(t, carry):
        for k in range(TOP_K):
            _row_copy(h_ref, t, xs_ref, dest_ref[0, k, t], sem).start()
        return carry

    lax.fori_loop(0, T, body, 0)
    for _ in range(TOP_K):
        pltpu.make_async_copy(h_ref, xs_ref.at[pl.ds(0, T), :], sem).wait()


def _dispatch(dest3, x, mod, nlat):
    n = x.shape[0]
    return pl.pallas_call(
        _dispatch_kernel,
        grid=(n // T,),
        in_specs=[
            pl.BlockSpec((1, TOP_K, T), lambda i: (i, 0, 0), memory_space=pltpu.SMEM),
            pl.BlockSpec((T, D), lambda i: (i, 0)),
            pl.BlockSpec((1, MOD_ROWS, D), lambda i: (i // nlat, 0, 0)),
        ],
        out_specs=pl.BlockSpec(memory_space=pl.ANY),
        out_shape=jax.ShapeDtypeStruct((n * TOP_K, D), F32),
        scratch_shapes=[pltpu.VMEM((T, D), F32), pltpu.SemaphoreType.DMA],
        compiler_params=_cparams(("arbitrary",)),
        name="moe_dispatch",
    )(dest3, x, mod)


def _expert_kernel(vt_ref, ve_ref, vlo_ref, vhi_ref, vnew_ref, nv_ref, xs_ref, wg_ref, wu_ref, wd_ref,
                   ys_ref, wgu_bf, wd_bf):
    v = pl.program_id(0)

    @pl.when(v < nv_ref[0])
    def _():
        @pl.when(vnew_ref[v] == 1)
        def _():
            wgu_bf[:, 0:EXPERT_FF] = wg_ref[0, 0].astype(BF16)
            wgu_bf[:, EXPERT_FF:2 * EXPERT_FF] = wu_ref[0, 0].astype(BF16)
            wd_bf[...] = wd_ref[0, 0].astype(BF16)

        h12 = _dot(xs_ref[...].astype(BF16), wgu_bf[...])
        a = (_silu(h12[:, 0:EXPERT_FF]) * h12[:, EXPERT_FF:2 * EXPERT_FF]).astype(BF16)
        y = _dot(a, wd_bf[...])
        lo = vlo_ref[v]
        hi = vhi_ref[v]
        whole = jnp.logical_and(lo == 0, hi == T)

        @pl.when(whole)
        def _():
            ys_ref[...] = y

        @pl.when(jnp.logical_not(whole))
        def _():
            row = lax.broadcasted_iota(I32, (T, 1), 0)
            mine = jnp.logical_and(row >= lo, row < hi)

            @pl.when(lo == 0)
            def _():
                ys_ref[...] = jnp.where(mine, y, 0.0)

            @pl.when(lo != 0)
            def _():
                ys_ref[...] = jnp.where(mine, y, ys_ref[...])


def _expert_ffn(sched, xs, w_gate, w_up, w_down, li):
    vt, ve, vlo, vhi, vnew, nv = sched
    nslots = xs.shape[0]
    nvis = vt.shape[0]
    w_in = lambda v, vt, ve, *_: (li, ve[v], 0, 0)
    rows = lambda v, vt, *_: (vt[v], 0)
    grid_spec = pltpu.PrefetchScalarGridSpec(
        num_scalar_prefetch=6,
        grid=(nvis,),
        in_specs=[
            pl.BlockSpec((T, D), rows),
            pl.BlockSpec((1, 1, D, EXPERT_FF), w_in),
            pl.BlockSpec((1, 1, D, EXPERT_FF), w_in),
            pl.BlockSpec((1, 1, EXPERT_FF, D), w_in),
        ],
        out_specs=pl.BlockSpec((T, D), rows),
        scratch_shapes=[pltpu.VMEM((D, 2 * EXPERT_FF), BF16), pltpu.VMEM((EXPERT_FF, D), BF16)],
    )
    return pl.pallas_call(
        _expert_kernel,
        grid_spec=grid_spec,
        out_shape=jax.ShapeDtypeStruct((nslots, D), F32),
        compiler_params=_cparams(("arbitrary",)),
        name="moe_experts",
    )(vt, ve, vlo, vhi, vnew, nv, xs, w_gate, w_up, w_down)


def _combine_kernel(dest_ref, ys_ref, x_ref, m_ref, gt_ref, sg_ref, su_ref, sd_ref, g_ref, b_ref,
                    o_ref, buf_ref, sem):
    def body(t, carry):
        for k in range(TOP_K):
            pltpu.make_async_copy(ys_ref.at[pl.ds(dest_ref[0, k, t], 1), :],
                                  buf_ref.at[k, pl.ds(t, 1), :], sem).start()
        return carry

    lax.fori_loop(0, T, body, 0)

    m = m_ref[0]
    x = x_ref[...]
    hb = (x * (1.0 + m[4:5, :]) + m[3:4, :]).astype(BF16)
    f = _dot((_silu(_dot(hb, sg_ref[...])) * _dot(hb, su_ref[...])).astype(BF16), sd_ref[...])

    for k in range(TOP_K):
        pltpu.make_async_copy(ys_ref.at[pl.ds(0, T), :], buf_ref.at[k], sem).wait()
    gt = gt_ref[...]
    for k in range(TOP_K):
        f = f + gt[:, k:k + 1] * buf_ref[k]
    o_ref[...] = _residual_ln(x, f, m, 5, g_ref[...], b_ref[...])


def _combine(dest3, ys, x, mod, gates_t, sg_bf, su_bf, sd_bf, g, b, nlat):
    n = x.shape[0]
    const2 = lambda i: (0, 0)
    return pl.pallas_call(
        _combine_kernel,
        grid=(n // T,),
        in_specs=[
            pl.BlockSpec((1, TOP_K, T), lambda i: (i, 0, 0), memory_space=pltpu.SMEM),
            pl.BlockSpec(memory_space=pl.ANY),
            pl.BlockSpec((T, D), lambda i: (i, 0)),
            pl.BlockSpec((1, MOD_ROWS, D), lambda i: (i // nlat, 0, 0)),
            pl.BlockSpec((T, TOP_K), lambda i: (i, 0)),
            pl.BlockSpec((D, EXPERT_FF), const2),
            pl.BlockSpec((D, EXPERT_FF), const2),
            pl.BlockSpec((EXPERT_FF, D), const2),
            pl.BlockSpec((1, D), const2),
            pl.BlockSpec((1, D), const2),
        ],
        out_specs=pl.BlockSpec((T, D), lambda i: (i, 0)),
        out_shape=jax.ShapeDtypeStruct((n, D), F32),
        scratch_shapes=[pltpu.VMEM((TOP_K, T, D), F32), pltpu.SemaphoreType.DMA],
        compiler_params=_cparams(("arbitrary",)),
        name="moe_combine",
    )(dest3, ys, x, mod, gates_t, sg_bf, su_bf, sd_bf, g, b)


def _slot_schedule(counts, n_slot_tiles):
    offs = jnp.cumsum(counts) - counts
    ends = offs + counts
    first_tile = offs // T
    last_tile = jnp.maximum(ends - 1, offs) // T
    nvis_e = jnp.where(counts > 0, last_tile - first_tile + 1, 0)
    vstart = jnp.cumsum(nvis_e) - nvis_e
    total = jnp.sum(nvis_e)
    nvis = n_slot_tiles + N_EXPERTS - 1
    v = jnp.arange(nvis, dtype=I32)
    experts = jnp.arange(N_EXPERTS, dtype=I32)
    e = jnp.minimum(jnp.sum((vstart + nvis_e)[None, :] <= v[:, None], axis=1), N_EXPERTS - 1).astype(I32)
    onehot = e[:, None] == experts[None, :]
    pick = lambda a: jnp.sum(jnp.where(onehot, a[None, :], 0), axis=1)
    tile = pick(first_tile) + (v - pick(vstart))
    lo = jnp.maximum(pick(offs), tile * T) - tile * T
    hi = jnp.minimum(pick(ends), (tile + 1) * T) - tile * T
    valid = v < total
    e_last = jnp.max(jnp.where(counts > 0, experts, 0))
    vt = jnp.where(valid, tile, n_slot_tiles - 1).astype(I32)
    ve = jnp.where(valid, e, e_last).astype(I32)
    vlo = jnp.where(valid, lo, 0).astype(I32)
    vhi = jnp.where(valid, hi, 0).astype(I32)
    vnew = jnp.concatenate([jnp.ones((1,), I32), (ve[1:] != ve[:-1]).astype(I32)])
    return offs, (vt, ve, vlo, vhi, vnew, total.astype(I32).reshape(1))


def _moe_layer(x, mod, rw_t, rb_col, w_gate, w_up, w_down, li, sg_bf, su_bf, sd_bf, g, b, nlat):
    n = x.shape[0]
    eidx, gates, rank, cnt = _router(x, mod, rw_t, rb_col, nlat)
    counts = cnt[:, 0].astype(I32)
    offs, sched = _slot_schedule(counts, n * TOP_K // T)
    experts = jnp.arange(N_EXPERTS, dtype=I32)
    dest = rank + jnp.sum(jnp.where(eidx[..., None] == experts, offs, 0), axis=-1)
    dest3 = dest.reshape(TOP_K, n // T, T).transpose(1, 0, 2)
    xs = _dispatch(dest3, x, mod, nlat)
    ys = _expert_ffn(sched, xs, w_gate, w_up, w_down, li)
    return _combine(dest3, ys, x, mod, gates.T, sg_bf, su_bf, sd_bf, g, b, nlat)


def _rope_tables(n_lat, n_ctx):
    rows = n_lat // GRID_W
    r, cidx = jnp.meshgrid(jnp.arange(rows), jnp.arange(GRID_W), indexing="ij")
    pos = jnp.stack([r.reshape(-1), cidx.reshape(-1)], axis=-1).astype(F32)
    nf = DIFF_QK_DIM // 4
    inv = ROPE_BASE ** (-jnp.arange(nf, dtype=F32) / nf)
    ang = jnp.broadcast_to(pos[:, :, None, None] * inv, (n_lat, 2, 2, nf)).reshape(n_lat, DIFF_QK_DIM)
    cos = jnp.concatenate([jnp.cos(ang), jnp.ones((n_ctx, DIFF_QK_DIM), F32)], axis=0)
    sin = jnp.concatenate([jnp.sin(ang), jnp.zeros((n_ctx, DIFF_QK_DIM), F32)], axis=0)
    return cos, sin


def kernel(x, c, ctx, c_ctx, ada_w, ada_b, ln_g, ln_b, ev_w_in, ev_w_out, pool_w, pool_scale, sgu_ln_g, sgu_ln_b, sgu_w, sgu_b, od_w_in, od_w_out, diff_lambda, diff_subln, conv_w, router_w, router_bias, exp_w_gate, exp_w_up, exp_w_down, sh_w_gate, sh_w_up, sh_w_down):
    bsz, n_lat, d = x.shape
    n_ctx = ctx.shape[1]
    assert bsz == 1 and d == D and n_lat % T == 0 and n_ctx % T == 0
    assert n_lat % ATT_TK == 0 and n_lat % ATT_TQ == 0 and n_lat % n_ctx == 0 and n_ctx % LANES == 0
    depth = ada_w.shape[0]
    nlat = n_lat // T

    xs = jnp.concatenate([x[0], ctx[0]], axis=0)
    mods = _ada_mods(jnp.stack([c[0], c_ctx], axis=1), ada_w, ada_b)
    cos, sin = _rope_tables(n_lat, n_ctx)
    eye = jnp.eye(T // SGU_CHUNK, dtype=F32)

    for li in range(depth):
        j = li // 2
        mod = mods[li]
        g0, b0 = ln_g[li, 0][None], ln_b[li, 0][None]
        g1, b1 = ln_g[li, 1][None], ln_b[li, 1][None]
        if li % 2 == 0:
            p = _mod_matmul(xs, mod, ev_w_in[j].astype(BF16), nlat, sh=0, sc=1, tn=EVEN_IN)
            sguw_bd = jnp.einsum("ab,hpq->hapbq", eye, sgu_w[j]).reshape(SGU_HEADS, T, T).astype(BF16)
            sgub_t = jnp.tile(sgu_b[j].T, (T // SGU_CHUNK, 1))
            xs = _even_mix(p, xs, mod, pool_w[j].astype(BF16), pool_scale[j], sgu_ln_g[j][None], sgu_ln_b[j][None],
                           sguw_bd, sgub_t, ev_w_out[j].astype(BF16), g0, b0, nlat)
        else:
            p = _mod_matmul(xs, mod, od_w_in[j].astype(BF16), nlat, sh=0, sc=1, tn=ODD_IN // 2)
            q, k, v = _rope_prep(p, cos, sin)
            lam_init = 0.8 - 0.6 * math.exp(-0.3 * li)
            on = _diff_attention(q, k, v, diff_lambda[j], diff_subln[j][None], n_lat, lam_init)
            xs = _odd_out(on, p, conv_w[j], xs, mod, od_w_out[j].astype(BF16), g0, b0, nlat)
        xs = _moe_layer(xs, mod, router_w[li].T, router_bias[li][:, None], exp_w_gate, exp_w_up,
                        exp_w_down, li, sh_w_gate[li].astype(BF16), sh_w_up[li].astype(BF16),
                        sh_w_down[li].astype(BF16), g1, b1, nlat)
    return xs[:n_lat][None]
```

```python
import functools
import math

import jax
import jax.numpy as jnp
from jax import lax
from jax.experimental import pallas as pl
from jax.experimental.pallas import tpu as pltpu

F32 = jnp.float32
BF16 = jnp.bfloat16
I32 = jnp.int32

D = 2048
DEPTH = 4
GRID_W = 64

POOL_GROUPS = 4
POOL_WINDOWS = (2, 4, 8, 16)
POOL_WIDTH = 1024
POOL_GROUP_DIM = 256
SGU_WIDTH = 1024
SGU_CHUNK = 128
SGU_HEADS = 8
EVEN_IN = 3072

DIFF_HEADS = 4
DIFF_QK_DIM = 128
DIFF_V_DIM = 256
DIFF_WIDTH = 1024
CONV_WIDTH = 1024
ODD_IN = 6144
ROPE_BASE = 10000.0

N_EXPERTS = 64
EXPERT_FF = 384
TOP_K = 8
N_GROUPS = 8
GROUP_SIZE = 8
TOPK_GROUPS = 4
ROUTED_SCALE = 2.5

ALPHA = (2 * DEPTH) ** 0.25
LN_EPS = 1e-6
RMS_EPS = 1e-5

T = 256
TS = 512
EXPERT_RB = 128
DMA_UNROLL = 16
HALO = 8
MOD_ROWS = 8
ADA_TN = 1536
ADA_UNROLL = 8
ATT_TQ = 512
ATT_TK = 512
ATT_RB = 64
ATT_NBUF = 3
LANES = 128
LOG2E = 1.4426950408889634
VMEM_LIMIT = 56 * 1024 * 1024


def _cparams(sem, vmem=VMEM_LIMIT, flags=None):
    return pltpu.CompilerParams(dimension_semantics=sem, vmem_limit_bytes=vmem, flags=flags)


def _sigmoid(x):
    return 1.0 / (1.0 + jnp.exp(-x))


def _silu(x):
    return x * _sigmoid(x)


def _layer_norm(v, g, b):
    mu = jnp.mean(v, axis=-1, keepdims=True)
    d = v - mu
    var = jnp.mean(d * d, axis=-1, keepdims=True)
    return d * lax.rsqrt(var + LN_EPS) * g + b


def _dot(a, b):
    return jnp.dot(a, b, preferred_element_type=F32)


def _dot_nt(a, b):
    return lax.dot_general(a, b, (((1,), (1,)), ((), ())), preferred_element_type=F32)


def _ada_kernel(c_ref, w_ref, b_ref, o_ref, s_ref):
    tn = w_ref.shape[2]
    s_ref[...] = _silu(c_ref[...])

    def body(m, carry):
        a0, a1 = carry
        for u in range(ADA_UNROLL):
            r = pl.multiple_of(m * (8 * ADA_UNROLL) + 8 * u, 8)
            w = w_ref[0, pl.ds(r, 8), :]
            s = s_ref[pl.ds(r, 8), :]
            a0 = a0 + w * s[:, 0:1]
            a1 = a1 + w * s[:, 1:2]
        return a0, a1

    z = jnp.zeros((8, tn), F32)
    a0, a1 = lax.fori_loop(0, D // (8 * ADA_UNROLL), body, (z, z))
    bias = b_ref[0]
    o_ref[0] = jnp.zeros((8, tn), F32)
    o_ref[0, 0:1, :] = jnp.sum(a0, axis=0, keepdims=True) + bias
    o_ref[0, 1:2, :] = jnp.sum(a1, axis=0, keepdims=True) + bias


def _ada_mods(c_cols, ada_w, ada_b):
    depth = ada_w.shape[0]
    n6 = ada_w.shape[2]
    out = pl.pallas_call(
        _ada_kernel,
        grid=(depth, n6 // ADA_TN),
        in_specs=[
            pl.BlockSpec((D, 2), lambda l, j: (0, 0)),
            pl.BlockSpec((1, D, ADA_TN), lambda l, j: (l, 0, j)),
            pl.BlockSpec((1, 1, ADA_TN), lambda l, j: (l, 0, j)),
        ],
        out_specs=pl.BlockSpec((1, 8, ADA_TN), lambda l, j: (l, 0, j)),
        out_shape=jax.ShapeDtypeStruct((depth, 8, n6), F32),
        scratch_shapes=[pltpu.VMEM((D, 2), F32)],
        compiler_params=_cparams(("arbitrary", "arbitrary")),
        name="ada_mods",
    )(c_cols, ada_w, ada_b.reshape(depth, 1, n6))
    mods = out[:, :2, :].reshape(depth, 2, 6, D)
    return jnp.pad(mods, ((0, 0), (0, 0), (0, MOD_ROWS - 6), (0, 0)))


def _modmm_kernel(x_ref, m_ref, w_ref, o_ref, *, sh, sc):
    m = m_ref[0]
    h = x_ref[...] * (1.0 + m[sc:sc + 1, :]) + m[sh:sh + 1, :]
    o_ref[...] = _dot(h.astype(BF16), w_ref[...])


def _mod_matmul(x, mod, w_bf, nlat, *, sh, sc, tn):
    n = x.shape[0]
    nout = w_bf.shape[1]
    return pl.pallas_call(
        functools.partial(_modmm_kernel, sh=sh, sc=sc),
        grid=(nout // tn, n // T),
        in_specs=[
            pl.BlockSpec((T, D), lambda j, i: (i, 0)),
            pl.BlockSpec((1, MOD_ROWS, D), lambda j, i: (i // nlat, 0, 0)),
            pl.BlockSpec((D, tn), lambda j, i: (0, j)),
        ],
        out_specs=pl.BlockSpec((T, tn), lambda j, i: (i, j)),
        out_shape=jax.ShapeDtypeStruct((n, nout), F32),
        compiler_params=_cparams(("arbitrary", "arbitrary")),
        name="mod_matmul",
    )(x, mod, w_bf)


def _seq_flags(i, nlat, ntiles):
    is_ctx = i >= nlat
    first = jnp.logical_or(i == 0, i == nlat)
    last = jnp.logical_or(i == nlat - 1, i == ntiles - 1)
    t0 = jnp.where(is_ctx, i - nlat, i) * T
    nseq = jnp.where(is_ctx, (ntiles - nlat) * T, nlat * T)
    return first, last, t0, nseq


def _residual_ln(x, y, m, gate_row, g, b):
    return _layer_norm(ALPHA * x + m[gate_row:gate_row + 1, :] * y, g, b)


def _even_mix_kernel(pp_ref, pprev_ref, pnext_ref, pu_ref, pv_ref, x_ref, m_ref,
                     poolw_ref, pscale_ref, slng_ref, slnb_ref, sguw_ref, sgub_ref,
                     wout_ref, g_ref, b_ref, o_ref, e_ref, cat_ref, *, nlat, ntiles):
    i = pl.program_id(0)
    first, last, t0, nseq = _seq_flags(i, nlat, ntiles)

    e_ref[0:HALO, :] = jnp.where(first, 0.0, pprev_ref[...])
    e_ref[HALO:HALO + T, :] = pp_ref[...]
    e_ref[HALO + T:HALO + T + HALO, :] = jnp.where(last, 0.0, pnext_ref[...])

    pos = t0 + lax.broadcasted_iota(I32, (T, 1), 0)
    for g in range(POOL_GROUPS):
        w = POOL_WINDOWS[g]
        lo, hi = w // 2, w - 1 - w // 2
        cols = slice(g * POOL_GROUP_DIM, (g + 1) * POOL_GROUP_DIM)
        tot = e_ref[HALO - lo:HALO - lo + T, cols]
        for d in range(-lo + 1, hi + 1):
            tot = tot + e_ref[HALO + d:HALO + d + T, cols]
        cnt = jnp.minimum(pos + hi, nseq - 1) - jnp.maximum(pos - lo, 0) + 1
        pooled = tot / cnt.astype(F32) - e_ref[HALO:HALO + T, cols]
        a = _dot(pooled.astype(BF16), poolw_ref[g]) * pscale_ref[g:g + 1, :]
        cat_ref[:, cols] = a.astype(BF16)

    zu = jax.nn.gelu(pu_ref[...])
    v = _layer_norm(jax.nn.gelu(pv_ref[...]), slng_ref[...], slnb_ref[...]).astype(BF16)
    for h in range(SGU_HEADS):
        cols = slice(h * SGU_CHUNK, (h + 1) * SGU_CHUNK)
        mixed = _dot(sguw_ref[h], v[:, cols]) + sgub_ref[:, h:h + 1]
        cat_ref[:, POOL_WIDTH + h * SGU_CHUNK:POOL_WIDTH + (h + 1) * SGU_CHUNK] = (zu[:, cols] * mixed).astype(BF16)

    y = _dot(cat_ref[...], wout_ref[...])
    o_ref[...] = _residual_ln(x_ref[...], y, m_ref[0], 2, g_ref[...], b_ref[...])


def _even_mix(p, x, mod, poolw_bf, pscale, slng, slnb, sguw_bd, sgub_t, wout_bf, g, b, nlat):
    n = x.shape[0]
    ntiles = n // T
    hb = T // HALO
    nhb = n // HALO
    kern = functools.partial(_even_mix_kernel, nlat=nlat, ntiles=ntiles)
    const2 = lambda i: (0, 0)
    const3 = lambda i: (0, 0, 0)
    return pl.pallas_call(
        kern,
        grid=(ntiles,),
        in_specs=[
            pl.BlockSpec((T, POOL_WIDTH), lambda i: (i, 0)),
            pl.BlockSpec((HALO, POOL_WIDTH), lambda i: (jnp.maximum(i * hb - 1, 0), 0)),
            pl.BlockSpec((HALO, POOL_WIDTH), lambda i: (jnp.minimum((i + 1) * hb, nhb - 1), 0)),
            pl.BlockSpec((T, SGU_WIDTH), lambda i: (i, 1)),
            pl.BlockSpec((T, SGU_WIDTH), lambda i: (i, 2)),
            pl.BlockSpec((T, D), lambda i: (i, 0)),
            pl.BlockSpec((1, MOD_ROWS, D), lambda i: (i // nlat, 0, 0)),
            pl.BlockSpec((POOL_GROUPS, POOL_GROUP_DIM, POOL_GROUP_DIM), const3),
            pl.BlockSpec((POOL_GROUPS, POOL_GROUP_DIM), const2),
            pl.BlockSpec((1, SGU_WIDTH), const2),
            pl.BlockSpec((1, SGU_WIDTH), const2),
            pl.BlockSpec((SGU_HEADS, T, T), const3),
            pl.BlockSpec((T, SGU_HEADS), const2),
            pl.BlockSpec((D, D), const2),
            pl.BlockSpec((1, D), const2),
            pl.BlockSpec((1, D), const2),
        ],
        out_specs=pl.BlockSpec((T, D), lambda i: (i, 0)),
        out_shape=jax.ShapeDtypeStruct((n, D), F32),
        scratch_shapes=[pltpu.VMEM((T + 2 * HALO, POOL_WIDTH), F32), pltpu.VMEM((T, D), BF16)],
        compiler_params=_cparams(("arbitrary",)),
        name="even_mix",
    )(p, p, p, p, p, x, mod, poolw_bf, pscale, slng, slnb, sguw_bd, sgub_t, wout_bf, g, b)


def _rope_kernel(pq_ref, pk_ref, pv_ref, cos_ref, sin_ref, q_ref, k_ref, v_ref):
    cos = cos_ref[...]
    sin = sin_ref[...]
    lane = lax.broadcasted_iota(I32, (1, DIFF_QK_DIM), 1)
    low_half = (lane % 64) < 32
    q_scale = DIFF_QK_DIM ** -0.5 * LOG2E

    def rope(x):
        rot = jnp.where(low_half, -pltpu.roll(x, 96, 1), pltpu.roll(x, 32, 1))
        return x * cos + rot * sin

    for j in range(DIFF_WIDTH // DIFF_QK_DIM):
        cols = slice(j * DIFF_QK_DIM, (j + 1) * DIFF_QK_DIM)
        q_ref[:, cols] = (rope(pq_ref[:, cols]) * q_scale).astype(BF16)
        k_ref[:, cols] = rope(pk_ref[:, cols]).astype(BF16)
    v_ref[...] = pv_ref[...].astype(BF16)


def _rope_prep(p, cos, sin):
    n = p.shape[0]
    blk = lambda c: pl.BlockSpec((T, DIFF_WIDTH), lambda i: (i, c))
    tab = pl.BlockSpec((T, DIFF_QK_DIM), lambda i: (i, 0))
    shp = jax.ShapeDtypeStruct((n, DIFF_WIDTH), BF16)
    return pl.pallas_call(
        _rope_kernel,
        grid=(n // T,),
        in_specs=[blk(0), blk(1), blk(2), tab, tab],
        out_specs=[blk(0), blk(0), blk(0)],
        out_shape=[shp, shp, shp],
        compiler_params=_cparams(("arbitrary",)),
        name="rope_prep",
    )(p, p, p, cos, sin)


def _attn_kernel(q_ref, k_ref, v_ref, dl_ref, sub_ref, o_ref, acc_ref, m_ref, l_ref, a_ref, s_ref, p_ref,
                 *, n_lat_chunks, ctx_start, n_ctx_keys, lam_init):
    m_ref[...] = jnp.full(m_ref.shape, -jnp.inf, F32)
    l_ref[...] = jnp.zeros(l_ref.shape, F32)
    acc_ref[...] = jnp.zeros(acc_ref.shape, F32)
    tq = q_ref.shape[0]

    def scores(start, size, b):
        for c in range(2):
            cols = slice(c * DIFF_QK_DIM, (c + 1) * DIFF_QK_DIM)
            s_ref[b, c, :, 0:size] = _dot_nt(q_ref[:, cols], k_ref[pl.ds(start, size), cols])

    def softmax(size, b):
        for c in range(2):
            for r in range(tq // ATT_RB):
                rows = slice(r * ATT_RB, (r + 1) * ATT_RB)
                s = s_ref[b, c, rows, 0:size]
                m_old = m_ref[c, rows, :]
                m_new = jnp.maximum(m_old, jnp.max(s, axis=-1, keepdims=True))
                alpha = jnp.exp2(m_old - m_new)
                lsum = None
                for j in range(size // LANES):
                    pj = jnp.exp2(s[:, j * LANES:(j + 1) * LANES] - m_new)
                    p_ref[b, c, rows, j * LANES:(j + 1) * LANES] = pj.astype(BF16)
                    lsum = pj if lsum is None else lsum + pj
                l_ref[c, rows, :] = alpha * l_ref[c, rows, :] + lsum
                m_ref[c, rows, :] = m_new
                a_ref[b, c, rows, :] = alpha

    def values(start, size, b):
        vv = v_ref[pl.ds(start, size), :]
        for c in range(2):
            alpha = a_ref[b, c]
            pv = _dot(p_ref[b, c, :, 0:size], vv)
            acc_ref[c] = jnp.concatenate([alpha, alpha], axis=1) * acc_ref[c] + pv

    n_chunks = n_lat_chunks + 1

    def chunk(j):
        if isinstance(j, int) and j == n_lat_chunks:
            return ctx_start, n_ctx_keys
        return (j * ATT_TK if isinstance(j, int) else pl.multiple_of(j * ATT_TK, ATT_TK)), ATT_TK

    def step(t, b):
        if not isinstance(t, int) or 0 <= t - 2 < n_chunks:
            values(*chunk(t - 2), (b - 2) % ATT_NBUF)
        if not isinstance(t, int) or t < n_chunks:
            scores(*chunk(t), b)
        if not isinstance(t, int) or 0 <= t - 1 < n_chunks:
            softmax(chunk(t - 1)[1] if isinstance(t, int) else ATT_TK, (b - 1) % ATT_NBUF)

    first_steady, n_steady = 2, max(n_lat_chunks - 2, 0)
    trips = n_steady // ATT_NBUF
    for t in range(first_steady):
        step(t, t % ATT_NBUF)

    def trip(i, carry):
        for u in range(ATT_NBUF):
            step(first_steady + ATT_NBUF * i + u, (first_steady + u) % ATT_NBUF)
        return carry

    lax.fori_loop(0, trips, trip, 0)
    for t in range(first_steady + trips * ATT_NBUF, n_chunks + 2):
        step(t, t % ATT_NBUF)

    dl = dl_ref[...]
    lam = (jnp.exp(jnp.sum(dl[0:1] * dl[1:2], axis=-1, keepdims=True))
           - jnp.exp(jnp.sum(dl[2:3] * dl[3:4], axis=-1, keepdims=True)) + lam_init)
    l0 = jnp.sum(l_ref[0], axis=-1, keepdims=True)
    l1 = jnp.sum(l_ref[1], axis=-1, keepdims=True)
    o = acc_ref[0] / l0 - lam * (acc_ref[1] / l1)
    o = o * lax.rsqrt(jnp.mean(o * o, axis=-1, keepdims=True) + RMS_EPS) * sub_ref[...]
    o_ref[...] = (o * (1.0 - lam_init)).astype(BF16)


def _attn_call(q, k, v, dl, subln, lam_init, *, tq, q_tile0, n_q_tiles, n_lat_chunks, ctx_start, n_ctx_keys):
    n = k.shape[0]
    kern = functools.partial(_attn_kernel, n_lat_chunks=n_lat_chunks, ctx_start=ctx_start,
                             n_ctx_keys=n_ctx_keys, lam_init=lam_init)
    resident = lambda: pl.BlockSpec((n, DIFF_V_DIM), lambda h, i: (0, h), pipeline_mode=pl.Buffered(1))
    return pl.pallas_call(
        kern,
        grid=(DIFF_HEADS, n_q_tiles),
        in_specs=[
            pl.BlockSpec((tq, DIFF_V_DIM), lambda h, i: (q_tile0 + i, h)),
            resident(),
            resident(),
            pl.BlockSpec((4, DIFF_QK_DIM), lambda h, i: (0, 0)),
            pl.BlockSpec((1, DIFF_V_DIM), lambda h, i: (0, 0)),
        ],
        out_specs=pl.BlockSpec((tq, DIFF_V_DIM), lambda h, i: (i, h)),
        out_shape=jax.ShapeDtypeStruct((n_q_tiles * tq, DIFF_WIDTH), BF16),
        scratch_shapes=[pltpu.VMEM((2, tq, DIFF_V_DIM), F32), pltpu.VMEM((2, tq, LANES), F32),
                        pltpu.VMEM((2, tq, LANES), F32), pltpu.VMEM((ATT_NBUF, 2, tq, LANES), F32),
                        pltpu.VMEM((ATT_NBUF, 2, tq, ATT_TK), F32),
                        pltpu.VMEM((ATT_NBUF, 2, tq, ATT_TK), BF16)],
        compiler_params=_cparams(("arbitrary", "arbitrary")),
        name="diff_attention",
    )(q, k, v, dl, subln)


def _diff_attention(q, k, v, dl, subln, n_lat, lam_init):
    n = q.shape[0]
    n_ctx = n - n_lat
    o_lat = _attn_call(q, k, v, dl, subln, lam_init, tq=ATT_TQ, q_tile0=0, n_q_tiles=n_lat // ATT_TQ,
                       n_lat_chunks=n_lat // ATT_TK, ctx_start=n_lat, n_ctx_keys=n_ctx)
    o_ctx = _attn_call(q, k, v, dl, subln, lam_init, tq=n_ctx, q_tile0=n_lat // n_ctx, n_q_tiles=1,
                       n_lat_chunks=0, ctx_start=n_lat, n_ctx_keys=n_ctx)
    return jnp.concatenate([o_lat, o_ctx], axis=0)


def _odd_out_kernel(on_ref, xin_ref, gb_ref, gc_ref, xinp_ref, gcp_ref, xinn_ref, gcn_ref,
                    cw_ref, x_ref, m_ref, wout_ref, g_ref, b_ref, o_ref, e_ref, cat_ref,
                    *, nlat, ntiles):
    i = pl.program_id(0)
    first, last, _, _ = _seq_flags(i, nlat, ntiles)
    u = gc_ref[...] * xin_ref[...]
    e_ref[0:HALO, :] = jnp.where(first, 0.0, gcp_ref[...] * xinp_ref[...])
    e_ref[HALO:HALO + T, :] = u
    e_ref[HALO + T:HALO + T + HALO, :] = jnp.where(last, 0.0, gcn_ref[...] * xinn_ref[...])
    cw = cw_ref[...]
    z = (cw[0:1, :] * e_ref[HALO - 1:HALO - 1 + T, :] + cw[1:2, :] * u
         + cw[2:3, :] * e_ref[HALO + 1:HALO + 1 + T, :])
    cat_ref[:, 0:DIFF_WIDTH] = on_ref[...]
    cat_ref[:, DIFF_WIDTH:D] = (gb_ref[...] * z).astype(BF16)
    y = _dot(cat_ref[...], wout_ref[...])
    o_ref[...] = _residual_ln(x_ref[...], y, m_ref[0], 2, g_ref[...], b_ref[...])


def _odd_out(on, p, conv_w, x, mod, wout_bf, g, b, nlat):
    n = x.shape[0]
    ntiles = n // T
    hb = T // HALO
    nhb = n // HALO
    kern = functools.partial(_odd_out_kernel, nlat=nlat, ntiles=ntiles)
    const2 = lambda i: (0, 0)
    blk = lambda c: pl.BlockSpec((T, CONV_WIDTH), lambda i: (i, c))
    prev = lambda c: pl.BlockSpec((HALO, CONV_WIDTH), lambda i: (jnp.maximum(i * hb - 1, 0), c))
    nxt = lambda c: pl.BlockSpec((HALO, CONV_WIDTH), lambda i: (jnp.minimum((i + 1) * hb, nhb - 1), c))
    return pl.pallas_call(
        kern,
        grid=(ntiles,),
        in_specs=[
            pl.BlockSpec((T, DIFF_WIDTH), lambda i: (i, 0)),
            blk(3), blk(4), blk(5), prev(3), prev(5), nxt(3), nxt(5),
            pl.BlockSpec((3, CONV_WIDTH), const2),
            pl.BlockSpec((T, D), lambda i: (i, 0)),
            pl.BlockSpec((1, MOD_ROWS, D), lambda i: (i // nlat, 0, 0)),
            pl.BlockSpec((D, D), const2),
            pl.BlockSpec((1, D), const2),
            pl.BlockSpec((1, D), const2),
        ],
        out_specs=pl.BlockSpec((T, D), lambda i: (i, 0)),
        out_shape=jax.ShapeDtypeStruct((n, D), F32),
        scratch_shapes=[pltpu.VMEM((T + 2 * HALO, CONV_WIDTH), F32), pltpu.VMEM((T, D), BF16)],
        compiler_params=_cparams(("arbitrary",)),
        name="odd_out",
    )(on, p, p, p, p, p, p, p, conv_w, x, mod, wout_bf, g, b)


def _router_kernel(x_ref, m_ref, rw_ref, rb_ref, eidx_ref, gate_ref, rank_ref, cnt_ref, base_ref):
    i = pl.program_id(0)

    @pl.when(i == 0)
    def _():
        base_ref[...] = jnp.zeros(base_ref.shape, F32)

    m = m_ref[0]
    h = x_ref[...] * (1.0 + m[4:5, :]) + m[3:4, :]
    h_hi = h.astype(BF16)
    h_lo = (h - h_hi.astype(F32)).astype(BF16)
    rw = rw_ref[...]
    rw_hi = rw.astype(BF16)
    rw_lo = (rw - rw_hi.astype(F32)).astype(BF16)
    logits = _dot_nt(rw_hi, h_hi) + (_dot_nt(rw_hi, h_lo) + _dot_nt(rw_lo, h_hi))
    scores = _sigmoid(logits)
    biased = scores + rb_ref[...]

    neg = -jnp.inf
    b3 = biased.reshape(N_GROUPS, GROUP_SIZE, T)
    io3 = lax.broadcasted_iota(I32, b3.shape, 1)
    m1 = jnp.max(b3, axis=1, keepdims=True)
    f1 = jnp.min(jnp.where(b3 == m1, io3, GROUP_SIZE), axis=1, keepdims=True)
    m2 = jnp.max(jnp.where(io3 == f1, neg, b3), axis=1, keepdims=True)
    gs = (m1 + m2).reshape(N_GROUPS, T)

    gio = lax.broadcasted_iota(I32, gs.shape, 0)
    gsel = jnp.zeros(gs.shape, F32)
    for _ in range(TOPK_GROUPS):
        mx = jnp.max(gs, axis=0, keepdims=True)
        f = jnp.min(jnp.where(gs == mx, gio, N_GROUPS), axis=0, keepdims=True)
        hit = gio == f
        gsel = jnp.where(hit, 1.0, gsel)
        gs = jnp.where(hit, neg, gs)
    masked = jnp.where(gsel.reshape(N_GROUPS, 1, T) > 0.5, b3, neg).reshape(N_EXPERTS, T)

    eio = lax.broadcasted_iota(I32, masked.shape, 0)
    hits, gates, eids = [], [], []
    onehot = jnp.zeros(masked.shape, F32)
    for _ in range(TOP_K):
        mx = jnp.max(masked, axis=0, keepdims=True)
        f = jnp.min(jnp.where(masked == mx, eio, N_EXPERTS), axis=0, keepdims=True)
        hit = eio == f
        hits.append(hit)
        eids.append(f)
        gates.append(jnp.sum(jnp.where(hit, scores, 0.0), axis=0, keepdims=True))
        onehot = jnp.where(hit, 1.0, onehot)
        masked = jnp.where(hit, neg, masked)
    gsum = gates[0]
    for gk in gates[1:]:
        gsum = gsum + gk

    r_io = lax.broadcasted_iota(I32, (T, T), 0)
    c_io = lax.broadcasted_iota(I32, (T, T), 1)
    upper = jnp.where(r_io < c_io, 1.0, 0.0).astype(BF16)
    before = base_ref[...] + _dot(onehot.astype(BF16), upper)
    for k in range(TOP_K):
        eidx_ref[k:k + 1, :] = eids[k]
        gate_ref[k:k + 1, :] = gates[k] / gsum * ROUTED_SCALE
        rank_ref[k:k + 1, :] = jnp.sum(jnp.where(hits[k], before, 0.0), axis=0, keepdims=True).astype(I32)
    base_new = base_ref[...] + jnp.sum(onehot, axis=1, keepdims=True)
    base_ref[...] = base_new
    cnt_ref[...] = jnp.broadcast_to(base_new, cnt_ref.shape)


def _router(x, mod, rw_t, rb_col, nlat):
    n = x.shape[0]
    kt = pl.BlockSpec((TOP_K, T), lambda i: (0, i))
    return pl.pallas_call(
        _router_kernel,
        grid=(n // T,),
        in_specs=[
            pl.BlockSpec((T, D), lambda i: (i, 0)),
            pl.BlockSpec((1, MOD_ROWS, D), lambda i: (i // nlat, 0, 0)),
            pl.BlockSpec((N_EXPERTS, D), lambda i: (0, 0)),
            pl.BlockSpec((N_EXPERTS, 1), lambda i: (0, 0)),
        ],
        out_specs=[kt, kt, kt, pl.BlockSpec((N_EXPERTS, 128), lambda i: (0, 0))],
        out_shape=[jax.ShapeDtypeStruct((TOP_K, n), I32), jax.ShapeDtypeStruct((TOP_K, n), F32),
                   jax.ShapeDtypeStruct((TOP_K, n), I32), jax.ShapeDtypeStruct((N_EXPERTS, 128), F32)],
        scratch_shapes=[pltpu.VMEM((N_EXPERTS, 1), F32)],
        compiler_params=_cparams(("arbitrary",)),
        name="moe_router",
    )(x, mod, rw_t, rb_col)


def _row_copy(src, src_row, dst, dst_row, sem):
    return pltpu.make_async_copy(src.at[pl.ds(src_row, 1), :], dst.at[pl.ds(dst_row, 1), :], sem)


def _dispatch_kernel(dest_ref, x_ref, m_ref, xs_ref, h_ref, sem):
    i = pl.program_id(0)
    slot = i % 2
    m = m_ref[0]
    h_ref[slot] = x_ref[...] * (1.0 + m[4:5, :]) + m[3:4, :]

    def body(tb, carry):
        for u in range(DMA_UNROLL):
            t = tb * DMA_UNROLL + u
            for k in range(TOP_K):
                _row_copy(h_ref.at[slot], t, xs_ref, dest_ref[0, k, t], sem.at[slot]).start(priority=k % 2)
        return carry

    lax.fori_loop(0, T // DMA_UNROLL, body, 0)

    def drain(s):
        for _ in range(TOP_K):
            pltpu.make_async_copy(h_ref.at[s], xs_ref.at[pl.ds(0, T), :], sem.at[s]).wait()

    @pl.when(i > 0)
    def _():
        drain(1 - slot)

    @pl.when(i == pl.num_programs(0) - 1)
    def _():
        drain(slot)


def _dispatch(dest3, x, mod, nlat):
    n = x.shape[0]
    return pl.pallas_call(
        _dispatch_kernel,
        grid=(n // T,),
        in_specs=[
            pl.BlockSpec((1, TOP_K, T), lambda i: (i, 0, 0), memory_space=pltpu.SMEM),
            pl.BlockSpec((T, D), lambda i: (i, 0)),
            pl.BlockSpec((1, MOD_ROWS, D), lambda i: (i // nlat, 0, 0)),
        ],
        out_specs=pl.BlockSpec(memory_space=pl.ANY),
        out_shape=jax.ShapeDtypeStruct((n * TOP_K, D), F32),
        scratch_shapes=[pltpu.VMEM((2, T, D), F32), pltpu.SemaphoreType.DMA((2,))],
        compiler_params=_cparams(("arbitrary",)),
        name="moe_dispatch",
    )(dest3, x, mod)


def _expert_kernel(vt_ref, ve_ref, vlo_ref, vhi_ref, vnew_ref, nv_ref, xs_ref, wg_ref, wu_ref, wd_ref,
                   ys_ref, wgu_bf, wd_bf):
    v = pl.program_id(0)

    @pl.when(v < nv_ref[0])
    def _():
        @pl.when(vnew_ref[v] == 1)
        def _():
            wgu_bf[:, 0:EXPERT_FF] = wg_ref[0, 0].astype(BF16)
            wgu_bf[:, EXPERT_FF:2 * EXPERT_FF] = wu_ref[0, 0].astype(BF16)
            wd_bf[...] = wd_ref[0, 0].astype(BF16)

        ys = []
        for r in range(TS // EXPERT_RB):
            xb = xs_ref[r * EXPERT_RB:(r + 1) * EXPERT_RB, :].astype(BF16)
            h12 = _dot(xb, wgu_bf[...])
            a = (_silu(h12[:, 0:EXPERT_FF]) * h12[:, EXPERT_FF:2 * EXPERT_FF]).astype(BF16)
            ys.append(_dot(a, wd_bf[...]))
        y = jnp.concatenate(ys, axis=0)
        lo = vlo_ref[v]
        hi = vhi_ref[v]
        whole = jnp.logical_and(lo == 0, hi == TS)

        @pl.when(whole)
        def _():
            ys_ref[...] = y

        @pl.when(jnp.logical_not(whole))
        def _():
            row = lax.broadcasted_iota(I32, (TS, 1), 0)
            mine = jnp.logical_and(row >= lo, row < hi)

            @pl.when(lo == 0)
            def _():
                ys_ref[...] = jnp.where(mine, y, 0.0)

            @pl.when(lo != 0)
            def _():
                ys_ref[...] = jnp.where(mine, y, ys_ref[...])


def _expert_ffn(sched, xs, w_gate, w_up, w_down, li):
    vt, ve, vlo, vhi, vnew, nv = sched
    nslots = xs.shape[0]
    nvis = vt.shape[0]
    w_in = lambda v, vt, ve, *_: (li, ve[v], 0, 0)
    rows = lambda v, vt, *_: (vt[v], 0)
    grid_spec = pltpu.PrefetchScalarGridSpec(
        num_scalar_prefetch=6,
        grid=(nvis,),
        in_specs=[
            pl.BlockSpec((TS, D), rows),
            pl.BlockSpec((1, 1, D, EXPERT_FF), w_in),
            pl.BlockSpec((1, 1, D, EXPERT_FF), w_in),
            pl.BlockSpec((1, 1, EXPERT_FF, D), w_in),
        ],
        out_specs=pl.BlockSpec((TS, D), rows),
        scratch_shapes=[pltpu.VMEM((D, 2 * EXPERT_FF), BF16), pltpu.VMEM((EXPERT_FF, D), BF16)],
    )
    return pl.pallas_call(
        _expert_kernel,
        grid_spec=grid_spec,
        out_shape=jax.ShapeDtypeStruct((nslots, D), F32),
        compiler_params=_cparams(("arbitrary",)),
        name="moe_experts",
    )(vt, ve, vlo, vhi, vnew, nv, xs, w_gate, w_up, w_down)


def _combine_kernel(dest_ref, ys_ref, x_ref, m_ref, gt_ref, sg_ref, su_ref, sd_ref, g_ref, b_ref,
                    o_ref, buf_ref, sem):
    def body(tb, carry):
        for u in range(DMA_UNROLL):
            t = tb * DMA_UNROLL + u
            for k in range(TOP_K):
                pltpu.make_async_copy(ys_ref.at[pl.ds(dest_ref[0, k, t], 1), :],
                                      buf_ref.at[k, pl.ds(t, 1), :], sem).start(priority=k % 2)
        return carry

    lax.fori_loop(0, T // DMA_UNROLL, body, 0)

    m = m_ref[0]
    x = x_ref[...]
    hb = (x * (1.0 + m[4:5, :]) + m[3:4, :]).astype(BF16)
    f = _dot((_silu(_dot(hb, sg_ref[...])) * _dot(hb, su_ref[...])).astype(BF16), sd_ref[...])

    for k in range(TOP_K):
        pltpu.make_async_copy(ys_ref.at[pl.ds(0, T), :], buf_ref.at[k], sem).wait()
    gt = gt_ref[...]
    for k in range(TOP_K):
        f = f + gt[:, k:k + 1] * buf_ref[k]
    o_ref[...] = _residual_ln(x, f, m, 5, g_ref[...], b_ref[...])


def _combine(dest3, ys, x, mod, gates_t, sg_bf, su_bf, sd_bf, g, b, nlat):
    n = x.shape[0]
    const2 = lambda i: (0, 0)
    return pl.pallas_call(
        _combine_kernel,
        grid=(n // T,),
        in_specs=[
            pl.BlockSpec((1, TOP_K, T), lambda i: (i, 0, 0), memory_space=pltpu.SMEM),
            pl.BlockSpec(memory_space=pl.ANY),
            pl.BlockSpec((T, D), lambda i: (i, 0)),
            pl.BlockSpec((1, MOD_ROWS, D), lambda i: (i // nlat, 0, 0)),
            pl.BlockSpec((T, TOP_K), lambda i: (i, 0)),
            pl.BlockSpec((D, EXPERT_FF), const2),
            pl.BlockSpec((D, EXPERT_FF), const2),
            pl.BlockSpec((EXPERT_FF, D), const2),
            pl.BlockSpec((1, D), const2),
            pl.BlockSpec((1, D), const2),
        ],
        out_specs=pl.BlockSpec((T, D), lambda i: (i, 0)),
        out_shape=jax.ShapeDtypeStruct((n, D), F32),
        scratch_shapes=[pltpu.VMEM((TOP_K, T, D), F32), pltpu.SemaphoreType.DMA],
        compiler_params=_cparams(("arbitrary",)),
        name="moe_combine",
    )(dest3, ys, x, mod, gates_t, sg_bf, su_bf, sd_bf, g, b)


def _slot_schedule(counts, n_slot_tiles):
    offs = jnp.cumsum(counts) - counts
    ends = offs + counts
    first_tile = offs // TS
    last_tile = jnp.maximum(ends - 1, offs) // TS
    nvis_e = jnp.where(counts > 0, last_tile - first_tile + 1, 0)
    vstart = jnp.cumsum(nvis_e) - nvis_e
    total = jnp.sum(nvis_e)
    nvis = n_slot_tiles + N_EXPERTS - 1
    v = jnp.arange(nvis, dtype=I32)
    experts = jnp.arange(N_EXPERTS, dtype=I32)
    e = jnp.minimum(jnp.sum((vstart + nvis_e)[None, :] <= v[:, None], axis=1), N_EXPERTS - 1).astype(I32)
    onehot = e[:, None] == experts[None, :]
    pick = lambda a: jnp.sum(jnp.where(onehot, a[None, :], 0), axis=1)
    tile = pick(first_tile) + (v - pick(vstart))
    lo = jnp.maximum(pick(offs), tile * TS) - tile * TS
    hi = jnp.minimum(pick(ends), (tile + 1) * TS) - tile * TS
    valid = v < total
    e_last = jnp.max(jnp.where(counts > 0, experts, 0))
    vt = jnp.where(valid, tile, n_slot_tiles - 1).astype(I32)
    ve = jnp.where(valid, e, e_last).astype(I32)
    vlo = jnp.where(valid, lo, 0).astype(I32)
    vhi = jnp.where(valid, hi, 0).astype(I32)
    vnew = jnp.concatenate([jnp.ones((1,), I32), (ve[1:] != ve[:-1]).astype(I32)])
    return offs, (vt, ve, vlo, vhi, vnew, total.astype(I32).reshape(1))


def _moe_layer(x, mod, rw_t, rb_col, w_gate, w_up, w_down, li, sg_bf, su_bf, sd_bf, g, b, nlat):
    n = x.shape[0]
    eidx, gates, rank, cnt = _router(x, mod, rw_t, rb_col, nlat)
    counts = cnt[:, 0].astype(I32)
    offs, sched = _slot_schedule(counts, n * TOP_K // TS)
    experts = jnp.arange(N_EXPERTS, dtype=I32)
    dest = rank + jnp.sum(jnp.where(eidx[..., None] == experts, offs, 0), axis=-1)
    dest3 = dest.reshape(TOP_K, n // T, T).transpose(1, 0, 2)
    xs = _dispatch(dest3, x, mod, nlat)
    ys = _expert_ffn(sched, xs, w_gate, w_up, w_down, li)
    return _combine(dest3, ys, x, mod, gates.T, sg_bf, su_bf, sd_bf, g, b, nlat)


def _rope_tables(n_lat, n_ctx):
    rows = n_lat // GRID_W
    r, cidx = jnp.meshgrid(jnp.arange(rows), jnp.arange(GRID_W), indexing="ij")
    pos = jnp.stack([r.reshape(-1), cidx.reshape(-1)], axis=-1).astype(F32)
    nf = DIFF_QK_DIM // 4
    inv = ROPE_BASE ** (-jnp.arange(nf, dtype=F32) / nf)
    ang = jnp.broadcast_to(pos[:, :, None, None] * inv, (n_lat, 2, 2, nf)).reshape(n_lat, DIFF_QK_DIM)
    cos = jnp.concatenate([jnp.cos(ang), jnp.ones((n_ctx, DIFF_QK_DIM), F32)], axis=0)
    sin = jnp.concatenate([jnp.sin(ang), jnp.zeros((n_ctx, DIFF_QK_DIM), F32)], axis=0)
    return cos, sin


def kernel(x, c, ctx, c_ctx, ada_w, ada_b, ln_g, ln_b, ev_w_in, ev_w_out, pool_w, pool_scale, sgu_ln_g, sgu_ln_b, sgu_w, sgu_b, od_w_in, od_w_out, diff_lambda, diff_subln, conv_w, router_w, router_bias, exp_w_gate, exp_w_up, exp_w_down, sh_w_gate, sh_w_up, sh_w_down):
    bsz, n_lat, d = x.shape
    n_ctx = ctx.shape[1]
    assert bsz == 1 and d == D and n_lat % T == 0 and n_ctx % T == 0
    assert n_lat % ATT_TK == 0 and n_lat % ATT_TQ == 0 and n_lat % n_ctx == 0 and n_ctx % LANES == 0
    depth = ada_w.shape[0]
    nlat = n_lat // T

    xs = jnp.concatenate([x[0], ctx[0]], axis=0)
    mods = _ada_mods(jnp.stack([c[0], c_ctx], axis=1), ada_w, ada_b)
    cos, sin = _rope_tables(n_lat, n_ctx)
    eye = jnp.eye(T // SGU_CHUNK, dtype=F32)

    for li in range(depth):
        j = li // 2
        mod = mods[li]
        g0, b0 = ln_g[li, 0][None], ln_b[li, 0][None]
        g1, b1 = ln_g[li, 1][None], ln_b[li, 1][None]
        if li % 2 == 0:
            p = _mod_matmul(xs, mod, ev_w_in[j].astype(BF16), nlat, sh=0, sc=1, tn=EVEN_IN)
            sguw_bd = jnp.einsum("ab,hpq->hapbq", eye, sgu_w[j]).reshape(SGU_HEADS, T, T).astype(BF16)
            sgub_t = jnp.tile(sgu_b[j].T, (T // SGU_CHUNK, 1))
            xs = _even_mix(p, xs, mod, pool_w[j].astype(BF16), pool_scale[j], sgu_ln_g[j][None], sgu_ln_b[j][None],
                           sguw_bd, sgub_t, ev_w_out[j].astype(BF16), g0, b0, nlat)
        else:
            p = _mod_matmul(xs, mod, od_w_in[j].astype(BF16), nlat, sh=0, sc=1, tn=ODD_IN // 2)
            q, k, v = _rope_prep(p, cos, sin)
            lam_init = 0.8 - 0.6 * math.exp(-0.3 * li)
            on = _diff_attention(q, k, v, diff_lambda[j], diff_subln[j][None], n_lat, lam_init)
            xs = _odd_out(on, p, conv_w[j], xs, mod, od_w_out[j].astype(BF16), g0, b0, nlat)
        xs = _moe_layer(xs, mod, router_w[li].T, router_bias[li][:, None], exp_w_gate, exp_w_up,
                        exp_w_down, li, sh_w_gate[li].astype(BF16), sh_w_up[li].astype(BF16),
                        sh_w_down[li].astype(BF16), g1, b1, nlat)
    return xs[:n_lat][None]
```

```python
import functools
import math

import jax
import jax.numpy as jnp
from jax import lax
from jax.experimental import pallas as pl
from jax.experimental.pallas import tpu as pltpu

F32 = jnp.float32
BF16 = jnp.bfloat16
I32 = jnp.int32
U32 = jnp.uint32

D = 2048
DEPTH = 4
GRID_W = 64

POOL_GROUPS = 4
POOL_WINDOWS = (2, 4, 8, 16)
POOL_WIDTH = 1024
POOL_GROUP_DIM = 256
SGU_WIDTH = 1024
SGU_CHUNK = 128
SGU_HEADS = 8
EVEN_IN = 3072

DIFF_HEADS = 4
DIFF_QK_DIM = 128
DIFF_V_DIM = 256
DIFF_WIDTH = 1024
CONV_WIDTH = 1024
ODD_IN = 6144
ROPE_BASE = 10000.0

N_EXPERTS = 64
EXPERT_FF = 384
TOP_K = 8
N_GROUPS = 8
GROUP_SIZE = 8
TOPK_GROUPS = 4
ROUTED_SCALE = 2.5

ALPHA = (2 * DEPTH) ** 0.25
LN_EPS = 1e-6
RMS_EPS = 1e-5

T = 256
TS = 512
EXPERT_RB = 128
DMA_UNROLL = 16
HALO = 8
MOD_ROWS = 8
ADA_TN = 1536
ADA_UNROLL = 8
ATT_TQ = 512
ATT_TK = 512
ATT_RB = 64
ATT_NBUF = 3
LANES = 128
LOG2E = 1.4426950408889634
VMEM_LIMIT = 56 * 1024 * 1024


def _cparams(sem, vmem=VMEM_LIMIT, flags=None):
    return pltpu.CompilerParams(dimension_semantics=sem, vmem_limit_bytes=vmem, flags=flags)


def _sigmoid(x):
    return 1.0 / (1.0 + jnp.exp(-x))


def _silu(x):
    return x * _sigmoid(x)


def _layer_norm(v, g, b):
    mu = jnp.mean(v, axis=-1, keepdims=True)
    d = v - mu
    var = jnp.mean(d * d, axis=-1, keepdims=True)
    return d * lax.rsqrt(var + LN_EPS) * g + b


def _dot(a, b):
    return jnp.dot(a, b, preferred_element_type=F32)


def _dot_nt(a, b):
    return lax.dot_general(a, b, (((1,), (1,)), ((), ())), preferred_element_type=F32)


def _ada_kernel(c_ref, w_ref, b_ref, o_ref, s_ref):
    tn = w_ref.shape[2]
    s_ref[...] = _silu(c_ref[...])

    def body(m, carry):
        a0, a1 = carry
        for u in range(ADA_UNROLL):
            r = pl.multiple_of(m * (8 * ADA_UNROLL) + 8 * u, 8)
            w = w_ref[0, pl.ds(r, 8), :]
            s = s_ref[pl.ds(r, 8), :]
            a0 = a0 + w * s[:, 0:1]
            a1 = a1 + w * s[:, 1:2]
        return a0, a1

    z = jnp.zeros((8, tn), F32)
    a0, a1 = lax.fori_loop(0, D // (8 * ADA_UNROLL), body, (z, z))
    bias = b_ref[0]
    o_ref[0] = jnp.zeros((8, tn), F32)
    o_ref[0, 0:1, :] = jnp.sum(a0, axis=0, keepdims=True) + bias
    o_ref[0, 1:2, :] = jnp.sum(a1, axis=0, keepdims=True) + bias


def _ada_mods(c_cols, ada_w, ada_b):
    depth = ada_w.shape[0]
    n6 = ada_w.shape[2]
    out = pl.pallas_call(
        _ada_kernel,
        grid=(depth, n6 // ADA_TN),
        in_specs=[
            pl.BlockSpec((D, 2), lambda l, j: (0, 0)),
            pl.BlockSpec((1, D, ADA_TN), lambda l, j: (l, 0, j)),
            pl.BlockSpec((1, 1, ADA_TN), lambda l, j: (l, 0, j)),
        ],
        out_specs=pl.BlockSpec((1, 8, ADA_TN), lambda l, j: (l, 0, j)),
        out_shape=jax.ShapeDtypeStruct((depth, 8, n6), F32),
        scratch_shapes=[pltpu.VMEM((D, 2), F32)],
        compiler_params=_cparams(("arbitrary", "arbitrary")),
        name="ada_mods",
    )(c_cols, ada_w, ada_b.reshape(depth, 1, n6))
    mods = out[:, :2, :].reshape(depth, 2, 6, D)
    return jnp.pad(mods, ((0, 0), (0, 0), (0, MOD_ROWS - 6), (0, 0)))


def _modmm_kernel(x_ref, m_ref, w_ref, o_ref, *, sh, sc):
    m = m_ref[0]
    h = x_ref[...] * (1.0 + m[sc:sc + 1, :]) + m[sh:sh + 1, :]
    o_ref[...] = _dot(h.astype(BF16), w_ref[...])


def _mod_matmul(x, mod, w_bf, nlat, *, sh, sc, tn):
    n = x.shape[0]
    nout = w_bf.shape[1]
    return pl.pallas_call(
        functools.partial(_modmm_kernel, sh=sh, sc=sc),
        grid=(nout // tn, n // T),
        in_specs=[
            pl.BlockSpec((T, D), lambda j, i: (i, 0)),
            pl.BlockSpec((1, MOD_ROWS, D), lambda j, i: (i // nlat, 0, 0)),
            pl.BlockSpec((D, tn), lambda j, i: (0, j)),
        ],
        out_specs=pl.BlockSpec((T, tn), lambda j, i: (i, j)),
        out_shape=jax.ShapeDtypeStruct((n, nout), F32),
        compiler_params=_cparams(("arbitrary", "arbitrary")),
        name="mod_matmul",
    )(x, mod, w_bf)


def _seq_flags(i, nlat, ntiles):
    is_ctx = i >= nlat
    first = jnp.logical_or(i == 0, i == nlat)
    last = jnp.logical_or(i == nlat - 1, i == ntiles - 1)
    t0 = jnp.where(is_ctx, i - nlat, i) * T
    nseq = jnp.where(is_ctx, (ntiles - nlat) * T, nlat * T)
    return first, last, t0, nseq


def _residual_ln(x, y, m, gate_row, g, b):
    return _layer_norm(ALPHA * x + m[gate_row:gate_row + 1, :] * y, g, b)


def _even_mix_kernel(pp_ref, pprev_ref, pnext_ref, pu_ref, pv_ref, x_ref, m_ref,
                     poolw_ref, pscale_ref, slng_ref, slnb_ref, sguw_ref, sgub_ref,
                     wout_ref, g_ref, b_ref, o_ref, e_ref, cat_ref, *, nlat, ntiles):
    i = pl.program_id(0)
    first, last, t0, nseq = _seq_flags(i, nlat, ntiles)

    e_ref[0:HALO, :] = jnp.where(first, 0.0, pprev_ref[...])
    e_ref[HALO:HALO + T, :] = pp_ref[...]
    e_ref[HALO + T:HALO + T + HALO, :] = jnp.where(last, 0.0, pnext_ref[...])

    pos = t0 + lax.broadcasted_iota(I32, (T, 1), 0)
    for g in range(POOL_GROUPS):
        w = POOL_WINDOWS[g]
        lo, hi = w // 2, w - 1 - w // 2
        cols = slice(g * POOL_GROUP_DIM, (g + 1) * POOL_GROUP_DIM)
        tot = e_ref[HALO - lo:HALO - lo + T, cols]
        for d in range(-lo + 1, hi + 1):
            tot = tot + e_ref[HALO + d:HALO + d + T, cols]
        cnt = jnp.minimum(pos + hi, nseq - 1) - jnp.maximum(pos - lo, 0) + 1
        pooled = tot / cnt.astype(F32) - e_ref[HALO:HALO + T, cols]
        a = _dot(pooled.astype(BF16), poolw_ref[g]) * pscale_ref[g:g + 1, :]
        cat_ref[:, cols] = a.astype(BF16)

    zu = jax.nn.gelu(pu_ref[...])
    v = _layer_norm(jax.nn.gelu(pv_ref[...]), slng_ref[...], slnb_ref[...]).astype(BF16)
    for h in range(SGU_HEADS):
        cols = slice(h * SGU_CHUNK, (h + 1) * SGU_CHUNK)
        mixed = _dot(sguw_ref[h], v[:, cols]) + sgub_ref[:, h:h + 1]
        cat_ref[:, POOL_WIDTH + h * SGU_CHUNK:POOL_WIDTH + (h + 1) * SGU_CHUNK] = (zu[:, cols] * mixed).astype(BF16)

    y = _dot(cat_ref[...], wout_ref[...])
    o_ref[...] = _residual_ln(x_ref[...], y, m_ref[0], 2, g_ref[...], b_ref[...])


def _even_mix(p, x, mod, poolw_bf, pscale, slng, slnb, sguw_bd, sgub_t, wout_bf, g, b, nlat):
    n = x.shape[0]
    ntiles = n // T
    hb = T // HALO
    nhb = n // HALO
    kern = functools.partial(_even_mix_kernel, nlat=nlat, ntiles=ntiles)
    const2 = lambda i: (0, 0)
    const3 = lambda i: (0, 0, 0)
    return pl.pallas_call(
        kern,
        grid=(ntiles,),
        in_specs=[
            pl.BlockSpec((T, POOL_WIDTH), lambda i: (i, 0)),
            pl.BlockSpec((HALO, POOL_WIDTH), lambda i: (jnp.maximum(i * hb - 1, 0), 0)),
            pl.BlockSpec((HALO, POOL_WIDTH), lambda i: (jnp.minimum((i + 1) * hb, nhb - 1), 0)),
            pl.BlockSpec((T, SGU_WIDTH), lambda i: (i, 1)),
            pl.BlockSpec((T, SGU_WIDTH), lambda i: (i, 2)),
            pl.BlockSpec((T, D), lambda i: (i, 0)),
            pl.BlockSpec((1, MOD_ROWS, D), lambda i: (i // nlat, 0, 0)),
            pl.BlockSpec((POOL_GROUPS, POOL_GROUP_DIM, POOL_GROUP_DIM), const3),
            pl.BlockSpec((POOL_GROUPS, POOL_GROUP_DIM), const2),
            pl.BlockSpec((1, SGU_WIDTH), const2),
            pl.BlockSpec((1, SGU_WIDTH), const2),
            pl.BlockSpec((SGU_HEADS, T, T), const3),
            pl.BlockSpec((T, SGU_HEADS), const2),
            pl.BlockSpec((D, D), const2),
            pl.BlockSpec((1, D), const2),
            pl.BlockSpec((1, D), const2),
        ],
        out_specs=pl.BlockSpec((T, D), lambda i: (i, 0)),
        out_shape=jax.ShapeDtypeStruct((n, D), F32),
        scratch_shapes=[pltpu.VMEM((T + 2 * HALO, POOL_WIDTH), F32), pltpu.VMEM((T, D), BF16)],
        compiler_params=_cparams(("arbitrary",)),
        name="even_mix",
    )(p, p, p, p, p, x, mod, poolw_bf, pscale, slng, slnb, sguw_bd, sgub_t, wout_bf, g, b)


def _rope_kernel(pq_ref, pk_ref, pv_ref, cos_ref, sin_ref, q_ref, k_ref, v_ref):
    cos = cos_ref[...]
    sin = sin_ref[...]
    lane = lax.broadcasted_iota(I32, (1, DIFF_QK_DIM), 1)
    low_half = (lane % 64) < 32
    q_scale = DIFF_QK_DIM ** -0.5 * LOG2E

    def rope(x):
        rot = jnp.where(low_half, -pltpu.roll(x, 96, 1), pltpu.roll(x, 32, 1))
        return x * cos + rot * sin

    for j in range(DIFF_WIDTH // DIFF_QK_DIM):
        cols = slice(j * DIFF_QK_DIM, (j + 1) * DIFF_QK_DIM)
        q_ref[:, cols] = (rope(pq_ref[:, cols]) * q_scale).astype(BF16)
        k_ref[:, cols] = rope(pk_ref[:, cols]).astype(BF16)
    v_ref[...] = pv_ref[...].astype(BF16)


def _rope_prep(p, cos, sin):
    n = p.shape[0]
    blk = lambda c: pl.BlockSpec((T, DIFF_WIDTH), lambda i: (i, c))
    tab = pl.BlockSpec((T, DIFF_QK_DIM), lambda i: (i, 0))
    shp = jax.ShapeDtypeStruct((n, DIFF_WIDTH), BF16)
    return pl.pallas_call(
        _rope_kernel,
        grid=(n // T,),
        in_specs=[blk(0), blk(1), blk(2), tab, tab],
        out_specs=[blk(0), blk(0), blk(0)],
        out_shape=[shp, shp, shp],
        compiler_params=_cparams(("arbitrary",)),
        name="rope_prep",
    )(p, p, p, cos, sin)


def _attn_kernel(q_ref, k_ref, v_ref, dl_ref, sub_ref, o_ref, acc_ref, m_ref, l_ref, a_ref, s_ref, p_ref,
                 *, n_lat_chunks, ctx_start, n_ctx_keys, lam_init):
    m_ref[...] = jnp.full(m_ref.shape, -jnp.inf, F32)
    l_ref[...] = jnp.zeros(l_ref.shape, F32)
    acc_ref[...] = jnp.zeros(acc_ref.shape, F32)
    tq = q_ref.shape[0]

    def scores(start, size, b):
        for c in range(2):
            cols = slice(c * DIFF_QK_DIM, (c + 1) * DIFF_QK_DIM)
            s_ref[b, c, :, 0:size] = _dot_nt(q_ref[:, cols], k_ref[pl.ds(start, size), cols])

    def softmax(size, b):
        for c in range(2):
            for r in range(tq // ATT_RB):
                rows = slice(r * ATT_RB, (r + 1) * ATT_RB)
                s = s_ref[b, c, rows, 0:size]
                m_old = m_ref[c, rows, :]
                m_new = jnp.maximum(m_old, jnp.max(s, axis=-1, keepdims=True))
                alpha = jnp.exp2(m_old - m_new)
                lsum = None
                for j in range(size // LANES):
                    pj = jnp.exp2(s[:, j * LANES:(j + 1) * LANES] - m_new)
                    p_ref[b, c, rows, j * LANES:(j + 1) * LANES] = pj.astype(BF16)
                    lsum = pj if lsum is None else lsum + pj
                l_ref[c, rows, :] = alpha * l_ref[c, rows, :] + lsum
                m_ref[c, rows, :] = m_new
                a_ref[b, c, rows, :] = alpha

    def values(start, size, b):
        vv = v_ref[pl.ds(start, size), :]
        for c in range(2):
            alpha = a_ref[b, c]
            pv = _dot(p_ref[b, c, :, 0:size], vv)
            acc_ref[c] = jnp.concatenate([alpha, alpha], axis=1) * acc_ref[c] + pv

    n_chunks = n_lat_chunks + 1

    def chunk(j):
        if isinstance(j, int) and j == n_lat_chunks:
            return ctx_start, n_ctx_keys
        return (j * ATT_TK if isinstance(j, int) else pl.multiple_of(j * ATT_TK, ATT_TK)), ATT_TK

    def step(t, b):
        if not isinstance(t, int) or 0 <= t - 2 < n_chunks:
            values(*chunk(t - 2), (b - 2) % ATT_NBUF)
        if not isinstance(t, int) or t < n_chunks:
            scores(*chunk(t), b)
        if not isinstance(t, int) or 0 <= t - 1 < n_chunks:
            softmax(chunk(t - 1)[1] if isinstance(t, int) else ATT_TK, (b - 1) % ATT_NBUF)

    first_steady, n_steady = 2, max(n_lat_chunks - 2, 0)
    trips = n_steady // ATT_NBUF
    for t in range(first_steady):
        step(t, t % ATT_NBUF)

    def trip(i, carry):
        for u in range(ATT_NBUF):
            step(first_steady + ATT_NBUF * i + u, (first_steady + u) % ATT_NBUF)
        return carry

    lax.fori_loop(0, trips, trip, 0)
    for t in range(first_steady + trips * ATT_NBUF, n_chunks + 2):
        step(t, t % ATT_NBUF)

    dl = dl_ref[...]
    lam = (jnp.exp(jnp.sum(dl[0:1] * dl[1:2], axis=-1, keepdims=True))
           - jnp.exp(jnp.sum(dl[2:3] * dl[3:4], axis=-1, keepdims=True)) + lam_init)
    l0 = jnp.sum(l_ref[0], axis=-1, keepdims=True)
    l1 = jnp.sum(l_ref[1], axis=-1, keepdims=True)
    o = acc_ref[0] / l0 - lam * (acc_ref[1] / l1)
    o = o * lax.rsqrt(jnp.mean(o * o, axis=-1, keepdims=True) + RMS_EPS) * sub_ref[...]
    o_ref[...] = (o * (1.0 - lam_init)).astype(BF16)


def _attn_call(q, k, v, dl, subln, lam_init, *, tq, q_tile0, n_q_tiles, n_lat_chunks, ctx_start, n_ctx_keys):
    n = k.shape[0]
    kern = functools.partial(_attn_kernel, n_lat_chunks=n_lat_chunks, ctx_start=ctx_start,
                             n_ctx_keys=n_ctx_keys, lam_init=lam_init)
    resident = lambda: pl.BlockSpec((n, DIFF_V_DIM), lambda h, i: (0, h), pipeline_mode=pl.Buffered(1))
    return pl.pallas_call(
        kern,
        grid=(DIFF_HEADS, n_q_tiles),
        in_specs=[
            pl.BlockSpec((tq, DIFF_V_DIM), lambda h, i: (q_tile0 + i, h)),
            resident(),
            resident(),
            pl.BlockSpec((4, DIFF_QK_DIM), lambda h, i: (0, 0)),
            pl.BlockSpec((1, DIFF_V_DIM), lambda h, i: (0, 0)),
        ],
        out_specs=pl.BlockSpec((tq, DIFF_V_DIM), lambda h, i: (i, h)),
        out_shape=jax.ShapeDtypeStruct((n_q_tiles * tq, DIFF_WIDTH), BF16),
        scratch_shapes=[pltpu.VMEM((2, tq, DIFF_V_DIM), F32), pltpu.VMEM((2, tq, LANES), F32),
                        pltpu.VMEM((2, tq, LANES), F32), pltpu.VMEM((ATT_NBUF, 2, tq, LANES), F32),
                        pltpu.VMEM((ATT_NBUF, 2, tq, ATT_TK), F32),
                        pltpu.VMEM((ATT_NBUF, 2, tq, ATT_TK), BF16)],
        compiler_params=_cparams(("arbitrary", "arbitrary")),
        name="diff_attention",
    )(q, k, v, dl, subln)


def _diff_attention(q, k, v, dl, subln, n_lat, lam_init):
    n = q.shape[0]
    n_ctx = n - n_lat
    o_lat = _attn_call(q, k, v, dl, subln, lam_init, tq=ATT_TQ, q_tile0=0, n_q_tiles=n_lat // ATT_TQ,
                       n_lat_chunks=n_lat // ATT_TK, ctx_start=n_lat, n_ctx_keys=n_ctx)
    o_ctx = _attn_call(q, k, v, dl, subln, lam_init, tq=n_ctx, q_tile0=n_lat // n_ctx, n_q_tiles=1,
                       n_lat_chunks=0, ctx_start=n_lat, n_ctx_keys=n_ctx)
    return jnp.concatenate([o_lat, o_ctx], axis=0)


def _odd_out_kernel(on_ref, xin_ref, gb_ref, gc_ref, xinp_ref, gcp_ref, xinn_ref, gcn_ref,
                    cw_ref, x_ref, m_ref, wout_ref, g_ref, b_ref, o_ref, e_ref, cat_ref,
                    *, nlat, ntiles):
    i = pl.program_id(0)
    first, last, _, _ = _seq_flags(i, nlat, ntiles)
    u = gc_ref[...] * xin_ref[...]
    e_ref[0:HALO, :] = jnp.where(first, 0.0, gcp_ref[...] * xinp_ref[...])
    e_ref[HALO:HALO + T, :] = u
    e_ref[HALO + T:HALO + T + HALO, :] = jnp.where(last, 0.0, gcn_ref[...] * xinn_ref[...])
    cw = cw_ref[...]
    z = (cw[0:1, :] * e_ref[HALO - 1:HALO - 1 + T, :] + cw[1:2, :] * u
         + cw[2:3, :] * e_ref[HALO + 1:HALO + 1 + T, :])
    cat_ref[:, 0:DIFF_WIDTH] = on_ref[...]
    cat_ref[:, DIFF_WIDTH:D] = (gb_ref[...] * z).astype(BF16)
    y = _dot(cat_ref[...], wout_ref[...])
    o_ref[...] = _residual_ln(x_ref[...], y, m_ref[0], 2, g_ref[...], b_ref[...])


def _odd_out(on, p, conv_w, x, mod, wout_bf, g, b, nlat):
    n = x.shape[0]
    ntiles = n // T
    hb = T // HALO
    nhb = n // HALO
    kern = functools.partial(_odd_out_kernel, nlat=nlat, ntiles=ntiles)
    const2 = lambda i: (0, 0)
    blk = lambda c: pl.BlockSpec((T, CONV_WIDTH), lambda i: (i, c))
    prev = lambda c: pl.BlockSpec((HALO, CONV_WIDTH), lambda i: (jnp.maximum(i * hb - 1, 0), c))
    nxt = lambda c: pl.BlockSpec((HALO, CONV_WIDTH), lambda i: (jnp.minimum((i + 1) * hb, nhb - 1), c))
    return pl.pallas_call(
        kern,
        grid=(ntiles,),
        in_specs=[
            pl.BlockSpec((T, DIFF_WIDTH), lambda i: (i, 0)),
            blk(3), blk(4), blk(5), prev(3), prev(5), nxt(3), nxt(5),
            pl.BlockSpec((3, CONV_WIDTH), const2),
            pl.BlockSpec((T, D), lambda i: (i, 0)),
            pl.BlockSpec((1, MOD_ROWS, D), lambda i: (i // nlat, 0, 0)),
            pl.BlockSpec((D, D), const2),
            pl.BlockSpec((1, D), const2),
            pl.BlockSpec((1, D), const2),
        ],
        out_specs=pl.BlockSpec((T, D), lambda i: (i, 0)),
        out_shape=jax.ShapeDtypeStruct((n, D), F32),
        scratch_shapes=[pltpu.VMEM((T + 2 * HALO, CONV_WIDTH), F32), pltpu.VMEM((T, D), BF16)],
        compiler_params=_cparams(("arbitrary",)),
        name="odd_out",
    )(on, p, p, p, p, p, p, p, conv_w, x, mod, wout_bf, g, b)


def _router_kernel(x_ref, m_ref, rw_ref, rb_ref, eidx_ref, gate_ref, rank_ref, cnt_ref, base_ref):
    i = pl.program_id(0)

    @pl.when(i == 0)
    def _():
        base_ref[...] = jnp.zeros(base_ref.shape, F32)

    m = m_ref[0]
    h = x_ref[...] * (1.0 + m[4:5, :]) + m[3:4, :]
    h_hi = h.astype(BF16)
    h_lo = (h - h_hi.astype(F32)).astype(BF16)
    rw = rw_ref[...]
    rw_hi = rw.astype(BF16)
    rw_lo = (rw - rw_hi.astype(F32)).astype(BF16)
    logits = _dot_nt(rw_hi, h_hi) + (_dot_nt(rw_hi, h_lo) + _dot_nt(rw_lo, h_hi))
    scores = _sigmoid(logits)
    biased = scores + rb_ref[...]

    neg = -jnp.inf
    b3 = biased.reshape(N_GROUPS, GROUP_SIZE, T)
    io3 = lax.broadcasted_iota(I32, b3.shape, 1)
    m1 = jnp.max(b3, axis=1, keepdims=True)
    f1 = jnp.min(jnp.where(b3 == m1, io3, GROUP_SIZE), axis=1, keepdims=True)
    m2 = jnp.max(jnp.where(io3 == f1, neg, b3), axis=1, keepdims=True)
    gs = (m1 + m2).reshape(N_GROUPS, T)

    gio = lax.broadcasted_iota(I32, gs.shape, 0)
    gsel = jnp.zeros(gs.shape, F32)
    for _ in range(TOPK_GROUPS):
        mx = jnp.max(gs, axis=0, keepdims=True)
        f = jnp.min(jnp.where(gs == mx, gio, N_GROUPS), axis=0, keepdims=True)
        hit = gio == f
        gsel = jnp.where(hit, 1.0, gsel)
        gs = jnp.where(hit, neg, gs)
    masked = jnp.where(gsel.reshape(N_GROUPS, 1, T) > 0.5, b3, neg).reshape(N_EXPERTS, T)

    eio = lax.broadcasted_iota(I32, masked.shape, 0)
    hits, gates, eids = [], [], []
    onehot = jnp.zeros(masked.shape, F32)
    for _ in range(TOP_K):
        mx = jnp.max(masked, axis=0, keepdims=True)
        f = jnp.min(jnp.where(masked == mx, eio, N_EXPERTS), axis=0, keepdims=True)
        hit = eio == f
        hits.append(hit)
        eids.append(f)
        gates.append(jnp.sum(jnp.where(hit, scores, 0.0), axis=0, keepdims=True))
        onehot = jnp.where(hit, 1.0, onehot)
        masked = jnp.where(hit, neg, masked)
    gsum = gates[0]
    for gk in gates[1:]:
        gsum = gsum + gk

    r_io = lax.broadcasted_iota(I32, (T, T), 0)
    c_io = lax.broadcasted_iota(I32, (T, T), 1)
    upper = jnp.where(r_io < c_io, 1.0, 0.0).astype(BF16)
    before = base_ref[...] + _dot(onehot.astype(BF16), upper)
    for k in range(TOP_K):
        eidx_ref[k:k + 1, :] = eids[k]
        gate_ref[k:k + 1, :] = gates[k] / gsum * ROUTED_SCALE
        rank_ref[k:k + 1, :] = jnp.sum(jnp.where(hits[k], before, 0.0), axis=0, keepdims=True).astype(I32)
    base_new = base_ref[...] + jnp.sum(onehot, axis=1, keepdims=True)
    base_ref[...] = base_new
    cnt_ref[...] = jnp.broadcast_to(base_new, cnt_ref.shape)


def _router(x, mod, rw_t, rb_col, nlat):
    n = x.shape[0]
    kt = pl.BlockSpec((TOP_K, T), lambda i: (0, i))
    return pl.pallas_call(
        _router_kernel,
        grid=(n // T,),
        in_specs=[
            pl.BlockSpec((T, D), lambda i: (i, 0)),
            pl.BlockSpec((1, MOD_ROWS, D), lambda i: (i // nlat, 0, 0)),
            pl.BlockSpec((N_EXPERTS, D), lambda i: (0, 0)),
            pl.BlockSpec((N_EXPERTS, 1), lambda i: (0, 0)),
        ],
        out_specs=[kt, kt, kt, pl.BlockSpec((N_EXPERTS, 128), lambda i: (0, 0))],
        out_shape=[jax.ShapeDtypeStruct((TOP_K, n), I32), jax.ShapeDtypeStruct((TOP_K, n), F32),
                   jax.ShapeDtypeStruct((TOP_K, n), I32), jax.ShapeDtypeStruct((N_EXPERTS, 128), F32)],
        scratch_shapes=[pltpu.VMEM((N_EXPERTS, 1), F32)],
        compiler_params=_cparams(("arbitrary",)),
        name="moe_router",
    )(x, mod, rw_t, rb_col)


def _row_copy(src, src_row, dst, dst_row, sem):
    return pltpu.make_async_copy(src.at[pl.ds(src_row, 1), :], dst.at[pl.ds(dst_row, 1), :], sem)


def _pack_bf16_pair(v):
    half = v.shape[1] // 2
    lo = pltpu.bitcast(v[:, :half].astype(BF16).astype(F32), U32)
    hi = pltpu.bitcast(v[:, half:].astype(BF16).astype(F32), U32)
    return hi | (lo >> 16)


def _unpack_bf16_pair(w):
    lo = pltpu.bitcast(w << 16, F32)
    hi = pltpu.bitcast(w & jnp.uint32(0xFFFF0000), F32)
    return lo, hi


def _dispatch_kernel(dest_ref, x_ref, m_ref, xs_ref, h_ref, sem):
    i = pl.program_id(0)
    slot = i % 2
    m = m_ref[0]
    h_ref[slot] = _pack_bf16_pair(x_ref[...] * (1.0 + m[4:5, :]) + m[3:4, :])

    def body(tb, carry):
        for u in range(DMA_UNROLL):
            t = tb * DMA_UNROLL + u
            for k in range(TOP_K):
                _row_copy(h_ref.at[slot], t, xs_ref, dest_ref[0, k, t], sem.at[slot]).start(priority=k % 2)
        return carry

    lax.fori_loop(0, T // DMA_UNROLL, body, 0)

    def drain(s):
        for _ in range(TOP_K):
            pltpu.make_async_copy(h_ref.at[s], xs_ref.at[pl.ds(0, T), :], sem.at[s]).wait()

    @pl.when(i > 0)
    def _():
        drain(1 - slot)

    @pl.when(i == pl.num_programs(0) - 1)
    def _():
        drain(slot)


def _dispatch(dest3, x, mod, nlat):
    n = x.shape[0]
    return pl.pallas_call(
        _dispatch_kernel,
        grid=(n // T,),
        in_specs=[
            pl.BlockSpec((1, TOP_K, T), lambda i: (i, 0, 0), memory_space=pltpu.SMEM),
            pl.BlockSpec((T, D), lambda i: (i, 0)),
            pl.BlockSpec((1, MOD_ROWS, D), lambda i: (i // nlat, 0, 0)),
        ],
        out_specs=pl.BlockSpec(memory_space=pl.ANY),
        out_shape=jax.ShapeDtypeStruct((n * TOP_K, D // 2), U32),
        scratch_shapes=[pltpu.VMEM((2, T, D // 2), U32), pltpu.SemaphoreType.DMA((2,))],
        compiler_params=_cparams(("arbitrary",)),
        name="moe_dispatch",
    )(dest3, x, mod)


def _expert_kernel(vt_ref, ve_ref, vlo_ref, vhi_ref, vnew_ref, nv_ref, xs_ref, wg_ref, wu_ref, wd_ref,
                   ys_ref, wgu_bf, wd_bf):
    v = pl.program_id(0)

    @pl.when(v < nv_ref[0])
    def _():
        @pl.when(vnew_ref[v] == 1)
        def _():
            wgu_bf[:, 0:EXPERT_FF] = wg_ref[0, 0].astype(BF16)
            wgu_bf[:, EXPERT_FF:2 * EXPERT_FF] = wu_ref[0, 0].astype(BF16)
            wd_bf[...] = wd_ref[0, 0].astype(BF16)

        ys = []
        for r in range(TS // EXPERT_RB):
            x_lo, x_hi = _unpack_bf16_pair(xs_ref[r * EXPERT_RB:(r + 1) * EXPERT_RB, :])
            xb = jnp.concatenate([x_lo.astype(BF16), x_hi.astype(BF16)], axis=1)
            h12 = _dot(xb, wgu_bf[...])
            a = (_silu(h12[:, 0:EXPERT_FF]) * h12[:, EXPERT_FF:2 * EXPERT_FF]).astype(BF16)
            ys.append(_pack_bf16_pair(_dot(a, wd_bf[...])))
        y = jnp.concatenate(ys, axis=0)
        lo = vlo_ref[v]
        hi = vhi_ref[v]
        whole = jnp.logical_and(lo == 0, hi == TS)

        @pl.when(whole)
        def _():
            ys_ref[...] = y

        @pl.when(jnp.logical_not(whole))
        def _():
            row = lax.broadcasted_iota(I32, (TS, 1), 0)
            mine = jnp.logical_and(row >= lo, row < hi)

            @pl.when(lo == 0)
            def _():
                ys_ref[...] = jnp.where(mine, y, jnp.uint32(0))

            @pl.when(lo != 0)
            def _():
                ys_ref[...] = jnp.where(mine, y, ys_ref[...])


def _expert_ffn(sched, xs, w_gate, w_up, w_down, li):
    vt, ve, vlo, vhi, vnew, nv = sched
    nslots = xs.shape[0]
    nvis = vt.shape[0]
    w_in = lambda v, vt, ve, *_: (li, ve[v], 0, 0)
    rows = lambda v, vt, *_: (vt[v], 0)
    grid_spec = pltpu.PrefetchScalarGridSpec(
        num_scalar_prefetch=6,
        grid=(nvis,),
        in_specs=[
            pl.BlockSpec((TS, D // 2), rows),
            pl.BlockSpec((1, 1, D, EXPERT_FF), w_in),
            pl.BlockSpec((1, 1, D, EXPERT_FF), w_in),
            pl.BlockSpec((1, 1, EXPERT_FF, D), w_in),
        ],
        out_specs=pl.BlockSpec((TS, D // 2), rows),
        scratch_shapes=[pltpu.VMEM((D, 2 * EXPERT_FF), BF16), pltpu.VMEM((EXPERT_FF, D), BF16)],
    )
    return pl.pallas_call(
        _expert_kernel,
        grid_spec=grid_spec,
        out_shape=jax.ShapeDtypeStruct((nslots, D // 2), U32),
        compiler_params=_cparams(("arbitrary",)),
        name="moe_experts",
    )(vt, ve, vlo, vhi, vnew, nv, xs, w_gate, w_up, w_down)


def _combine_kernel(dest_ref, ys_ref, x_ref, m_ref, gt_ref, sg_ref, su_ref, sd_ref, g_ref, b_ref,
                    o_ref, buf_ref, sem):
    def body(tb, carry):
        for u in range(DMA_UNROLL):
            t = tb * DMA_UNROLL + u
            for k in range(TOP_K):
                pltpu.make_async_copy(ys_ref.at[pl.ds(dest_ref[0, k, t], 1), :],
                                      buf_ref.at[k, pl.ds(t, 1), :], sem).start(priority=k % 2)
        return carry

    lax.fori_loop(0, T // DMA_UNROLL, body, 0)

    m = m_ref[0]
    x = x_ref[...]
    hb = (x * (1.0 + m[4:5, :]) + m[3:4, :]).astype(BF16)
    f = _dot((_silu(_dot(hb, sg_ref[...])) * _dot(hb, su_ref[...])).astype(BF16), sd_ref[...])

    for k in range(TOP_K):
        pltpu.make_async_copy(ys_ref.at[pl.ds(0, T), :], buf_ref.at[k], sem).wait()
    gt = gt_ref[...]
    f_lo = f[:, :D // 2]
    f_hi = f[:, D // 2:]
    for k in range(TOP_K):
        y_lo, y_hi = _unpack_bf16_pair(buf_ref[k])
        f_lo = f_lo + gt[:, k:k + 1] * y_lo
        f_hi = f_hi + gt[:, k:k + 1] * y_hi
    f = jnp.concatenate([f_lo, f_hi], axis=1)
    o_ref[...] = _residual_ln(x, f, m, 5, g_ref[...], b_ref[...])


def _combine(dest3, ys, x, mod, gates_t, sg_bf, su_bf, sd_bf, g, b, nlat):
    n = x.shape[0]
    const2 = lambda i: (0, 0)
    return pl.pallas_call(
        _combine_kernel,
        grid=(n // T,),
        in_specs=[
            pl.BlockSpec((1, TOP_K, T), lambda i: (i, 0, 0), memory_space=pltpu.SMEM),
            pl.BlockSpec(memory_space=pl.ANY),
            pl.BlockSpec((T, D), lambda i: (i, 0)),
            pl.BlockSpec((1, MOD_ROWS, D), lambda i: (i // nlat, 0, 0)),
            pl.BlockSpec((T, TOP_K), lambda i: (i, 0)),
            pl.BlockSpec((D, EXPERT_FF), const2),
            pl.BlockSpec((D, EXPERT_FF), const2),
            pl.BlockSpec((EXPERT_FF, D), const2),
            pl.BlockSpec((1, D), const2),
            pl.BlockSpec((1, D), const2),
        ],
        out_specs=pl.BlockSpec((T, D), lambda i: (i, 0)),
        out_shape=jax.ShapeDtypeStruct((n, D), F32),
        scratch_shapes=[pltpu.VMEM((TOP_K, T, D // 2), U32), pltpu.SemaphoreType.DMA],
        compiler_params=_cparams(("arbitrary",)),
        name="moe_combine",
    )(dest3, ys, x, mod, gates_t, sg_bf, su_bf, sd_bf, g, b)


def _slot_schedule(counts, n_slot_tiles):
    offs = jnp.cumsum(counts) - counts
    ends = offs + counts
    first_tile = offs // TS
    last_tile = jnp.maximum(ends - 1, offs) // TS
    nvis_e = jnp.where(counts > 0, last_tile - first_tile + 1, 0)
    vstart = jnp.cumsum(nvis_e) - nvis_e
    total = jnp.sum(nvis_e)
    nvis = n_slot_tiles + N_EXPERTS - 1
    v = jnp.arange(nvis, dtype=I32)
    experts = jnp.arange(N_EXPERTS, dtype=I32)
    e = jnp.minimum(jnp.sum((vstart + nvis_e)[None, :] <= v[:, None], axis=1), N_EXPERTS - 1).astype(I32)
    onehot = e[:, None] == experts[None, :]
    pick = lambda a: jnp.sum(jnp.where(onehot, a[None, :], 0), axis=1)
    tile = pick(first_tile) + (v - pick(vstart))
    lo = jnp.maximum(pick(offs), tile * TS) - tile * TS
    hi = jnp.minimum(pick(ends), (tile + 1) * TS) - tile * TS
    valid = v < total
    e_last = jnp.max(jnp.where(counts > 0, experts, 0))
    vt = jnp.where(valid, tile, n_slot_tiles - 1).astype(I32)
    ve = jnp.where(valid, e, e_last).astype(I32)
    vlo = jnp.where(valid, lo, 0).astype(I32)
    vhi = jnp.where(valid, hi, 0).astype(I32)
    vnew = jnp.concatenate([jnp.ones((1,), I32), (ve[1:] != ve[:-1]).astype(I32)])
    return offs, (vt, ve, vlo, vhi, vnew, total.astype(I32).reshape(1))


def _moe_layer(x, mod, rw_t, rb_col, w_gate, w_up, w_down, li, sg_bf, su_bf, sd_bf, g, b, nlat):
    n = x.shape[0]
    eidx, gates, rank, cnt = _router(x, mod, rw_t, rb_col, nlat)
    counts = cnt[:, 0].astype(I32)
    offs, sched = _slot_schedule(counts, n * TOP_K // TS)
    experts = jnp.arange(N_EXPERTS, dtype=I32)
    dest = rank + jnp.sum(jnp.where(eidx[..., None] == experts, offs, 0), axis=-1)
    dest3 = dest.reshape(TOP_K, n // T, T).transpose(1, 0, 2)
    xs = _dispatch(dest3, x, mod, nlat)
    ys = _expert_ffn(sched, xs, w_gate, w_up, w_down, li)
    return _combine(dest3, ys, x, mod, gates.T, sg_bf, su_bf, sd_bf, g, b, nlat)


def _rope_tables(n_lat, n_ctx):
    rows = n_lat // GRID_W
    r, cidx = jnp.meshgrid(jnp.arange(rows), jnp.arange(GRID_W), indexing="ij")
    pos = jnp.stack([r.reshape(-1), cidx.reshape(-1)], axis=-1).astype(F32)
    nf = DIFF_QK_DIM // 4
    inv = ROPE_BASE ** (-jnp.arange(nf, dtype=F32) / nf)
    ang = jnp.broadcast_to(pos[:, :, None, None] * inv, (n_lat, 2, 2, nf)).reshape(n_lat, DIFF_QK_DIM)
    cos = jnp.concatenate([jnp.cos(ang), jnp.ones((n_ctx, DIFF_QK_DIM), F32)], axis=0)
    sin = jnp.concatenate([jnp.sin(ang), jnp.zeros((n_ctx, DIFF_QK_DIM), F32)], axis=0)
    return cos, sin


def kernel(x, c, ctx, c_ctx, ada_w, ada_b, ln_g, ln_b, ev_w_in, ev_w_out, pool_w, pool_scale, sgu_ln_g, sgu_ln_b, sgu_w, sgu_b, od_w_in, od_w_out, diff_lambda, diff_subln, conv_w, router_w, router_bias, exp_w_gate, exp_w_up, exp_w_down, sh_w_gate, sh_w_up, sh_w_down):
    bsz, n_lat, d = x.shape
    n_ctx = ctx.shape[1]
    assert bsz == 1 and d == D and n_lat % T == 0 and n_ctx % T == 0
    assert n_lat % ATT_TK == 0 and n_lat % ATT_TQ == 0 and n_lat % n_ctx == 0 and n_ctx % LANES == 0
    depth = ada_w.shape[0]
    nlat = n_lat // T

    xs = jnp.concatenate([x[0], ctx[0]], axis=0)
    mods = _ada_mods(jnp.stack([c[0], c_ctx], axis=1), ada_w, ada_b)
    cos, sin = _rope_tables(n_lat, n_ctx)
    eye = jnp.eye(T // SGU_CHUNK, dtype=F32)

    for li in range(depth):
        j = li // 2
        mod = mods[li]
        g0, b0 = ln_g[li, 0][None], ln_b[li, 0][None]
        g1, b1 = ln_g[li, 1][None], ln_b[li, 1][None]
        if li % 2 == 0:
            p = _mod_matmul(xs, mod, ev_w_in[j].astype(BF16), nlat, sh=0, sc=1, tn=EVEN_IN)
            sguw_bd = jnp.einsum("ab,hpq->hapbq", eye, sgu_w[j]).reshape(SGU_HEADS, T, T).astype(BF16)
            sgub_t = jnp.tile(sgu_b[j].T, (T // SGU_CHUNK, 1))
            xs = _even_mix(p, xs, mod, pool_w[j].astype(BF16), pool_scale[j], sgu_ln_g[j][None], sgu_ln_b[j][None],
                           sguw_bd, sgub_t, ev_w_out[j].astype(BF16), g0, b0, nlat)
        else:
            p = _mod_matmul(xs, mod, od_w_in[j].astype(BF16), nlat, sh=0, sc=1, tn=ODD_IN // 2)
            q, k, v = _rope_prep(p, cos, sin)
            lam_init = 0.8 - 0.6 * math.exp(-0.3 * li)
            on = _diff_attention(q, k, v, diff_lambda[j], diff_subln[j][None], n_lat, lam_init)
            xs = _odd_out(on, p, conv_w[j], xs, mod, od_w_out[j].astype(BF16), g0, b0, nlat)
        xs = _moe_layer(xs, mod, router_w[li].T, router_bias[li][:, None], exp_w_gate, exp_w_up,
                        exp_w_down, li, sh_w_gate[li].astype(BF16), sh_w_up[li].astype(BF16),
                        sh_w_down[li].astype(BF16), g1, b1, nlat)
    return xs[:n_lat][None]
```

```python
import functools
import math

import jax
import jax.numpy as jnp
from jax import lax
from jax.experimental import pallas as pl
from jax.experimental.pallas import tpu as pltpu

F32 = jnp.float32
BF16 = jnp.bfloat16
I32 = jnp.int32
U32 = jnp.uint32

D = 2048
DEPTH = 4
GRID_W = 64

POOL_GROUPS = 4
POOL_WINDOWS = (2, 4, 8, 16)
POOL_WIDTH = 1024
POOL_GROUP_DIM = 256
SGU_WIDTH = 1024
SGU_CHUNK = 128
SGU_HEADS = 8
EVEN_IN = 3072

DIFF_HEADS = 4
DIFF_QK_DIM = 128
DIFF_V_DIM = 256
DIFF_WIDTH = 1024
CONV_WIDTH = 1024
ODD_IN = 6144
ROPE_BASE = 10000.0

N_EXPERTS = 64
EXPERT_FF = 384
TOP_K = 8
N_GROUPS = 8
GROUP_SIZE = 8
TOPK_GROUPS = 4
ROUTED_SCALE = 2.5

ALPHA = (2 * DEPTH) ** 0.25
LN_EPS = 1e-6
RMS_EPS = 1e-5

T = 256
TS = 512
EXPERT_RB = 128
DMA_UNROLL = 16
HALO = 8
MOD_ROWS = 8
ADA_TN = 1536
ADA_UNROLL = 8
ATT_TQ = 512
ATT_TK = 512
ATT_RB = 64
ATT_NBUF = 3
LANES = 128
LOG2E = 1.4426950408889634
VMEM_LIMIT = 56 * 1024 * 1024


def _cparams(sem, vmem=VMEM_LIMIT, flags=None):
    return pltpu.CompilerParams(dimension_semantics=sem, vmem_limit_bytes=vmem, flags=flags)


def _sigmoid(x):
    return 1.0 / (1.0 + jnp.exp(-x))


def _silu(x):
    return x * _sigmoid(x)


def _layer_norm(v, g, b):
    mu = jnp.mean(v, axis=-1, keepdims=True)
    d = v - mu
    var = jnp.mean(d * d, axis=-1, keepdims=True)
    return d * lax.rsqrt(var + LN_EPS) * g + b


def _dot(a, b):
    return jnp.dot(a, b, preferred_element_type=F32)


def _dot_nt(a, b):
    return lax.dot_general(a, b, (((1,), (1,)), ((), ())), preferred_element_type=F32)


def _ada_kernel(c_ref, w_ref, b_ref, o_ref, s_ref):
    tn = w_ref.shape[2]
    s_ref[...] = _silu(c_ref[...])

    def body(m, carry):
        a0, a1 = carry
        for u in range(ADA_UNROLL):
            r = pl.multiple_of(m * (8 * ADA_UNROLL) + 8 * u, 8)
            w = w_ref[0, pl.ds(r, 8), :]
            s = s_ref[pl.ds(r, 8), :]
            a0 = a0 + w * s[:, 0:1]
            a1 = a1 + w * s[:, 1:2]
        return a0, a1

    z = jnp.zeros((8, tn), F32)
    a0, a1 = lax.fori_loop(0, D // (8 * ADA_UNROLL), body, (z, z))
    bias = b_ref[0]
    o_ref[0] = jnp.zeros((8, tn), F32)
    o_ref[0, 0:1, :] = jnp.sum(a0, axis=0, keepdims=True) + bias
    o_ref[0, 1:2, :] = jnp.sum(a1, axis=0, keepdims=True) + bias


def _ada_mods(c_cols, ada_w, ada_b):
    depth = ada_w.shape[0]
    n6 = ada_w.shape[2]
    out = pl.pallas_call(
        _ada_kernel,
        grid=(depth, n6 // ADA_TN),
        in_specs=[
            pl.BlockSpec((D, 2), lambda l, j: (0, 0)),
            pl.BlockSpec((1, D, ADA_TN), lambda l, j: (l, 0, j)),
            pl.BlockSpec((1, 1, ADA_TN), lambda l, j: (l, 0, j)),
        ],
        out_specs=pl.BlockSpec((1, 8, ADA_TN), lambda l, j: (l, 0, j)),
        out_shape=jax.ShapeDtypeStruct((depth, 8, n6), F32),
        scratch_shapes=[pltpu.VMEM((D, 2), F32)],
        compiler_params=_cparams(("arbitrary", "arbitrary")),
        name="ada_mods",
    )(c_cols, ada_w, ada_b.reshape(depth, 1, n6))
    mods = out[:, :2, :].reshape(depth, 2, 6, D)
    return jnp.pad(mods, ((0, 0), (0, 0), (0, MOD_ROWS - 6), (0, 0)))


def _modmm_kernel(x_ref, m_ref, w_ref, o_ref, *, sh, sc):
    m = m_ref[0]
    h = x_ref[...] * (1.0 + m[sc:sc + 1, :]) + m[sh:sh + 1, :]
    o_ref[...] = _dot(h.astype(BF16), w_ref[...])


def _mod_matmul(x, mod, w_bf, nlat, *, sh, sc, tn):
    n = x.shape[0]
    nout = w_bf.shape[1]
    return pl.pallas_call(
        functools.partial(_modmm_kernel, sh=sh, sc=sc),
        grid=(nout // tn, n // T),
        in_specs=[
            pl.BlockSpec((T, D), lambda j, i: (i, 0)),
            pl.BlockSpec((1, MOD_ROWS, D), lambda j, i: (i // nlat, 0, 0)),
            pl.BlockSpec((D, tn), lambda j, i: (0, j)),
        ],
        out_specs=pl.BlockSpec((T, tn), lambda j, i: (i, j)),
        out_shape=jax.ShapeDtypeStruct((n, nout), F32),
        compiler_params=_cparams(("arbitrary", "arbitrary")),
        name="mod_matmul",
    )(x, mod, w_bf)


def _seq_flags(i, nlat, ntiles):
    is_ctx = i >= nlat
    first = jnp.logical_or(i == 0, i == nlat)
    last = jnp.logical_or(i == nlat - 1, i == ntiles - 1)
    t0 = jnp.where(is_ctx, i - nlat, i) * T
    nseq = jnp.where(is_ctx, (ntiles - nlat) * T, nlat * T)
    return first, last, t0, nseq


def _residual_ln(x, y, m, gate_row, g, b):
    return _layer_norm(ALPHA * x + m[gate_row:gate_row + 1, :] * y, g, b)


def _even_mix_kernel(pp_ref, pprev_ref, pnext_ref, pu_ref, pv_ref, x_ref, m_ref,
                     poolw_ref, pscale_ref, slng_ref, slnb_ref, sguw_ref, sgub_ref,
                     wout_ref, g_ref, b_ref, o_ref, e_ref, cat_ref, *, nlat, ntiles):
    i = pl.program_id(0)
    first, last, t0, nseq = _seq_flags(i, nlat, ntiles)

    e_ref[0:HALO, :] = jnp.where(first, 0.0, pprev_ref[...])
    e_ref[HALO:HALO + T, :] = pp_ref[...]
    e_ref[HALO + T:HALO + T + HALO, :] = jnp.where(last, 0.0, pnext_ref[...])

    pos = t0 + lax.broadcasted_iota(I32, (T, 1), 0)
    for g in range(POOL_GROUPS):
        w = POOL_WINDOWS[g]
        lo, hi = w // 2, w - 1 - w // 2
        cols = slice(g * POOL_GROUP_DIM, (g + 1) * POOL_GROUP_DIM)
        tot = e_ref[HALO - lo:HALO - lo + T, cols]
        for d in range(-lo + 1, hi + 1):
            tot = tot + e_ref[HALO + d:HALO + d + T, cols]
        cnt = jnp.minimum(pos + hi, nseq - 1) - jnp.maximum(pos - lo, 0) + 1
        pooled = tot / cnt.astype(F32) - e_ref[HALO:HALO + T, cols]
        a = _dot(pooled.astype(BF16), poolw_ref[g]) * pscale_ref[g:g + 1, :]
        cat_ref[:, cols] = a.astype(BF16)

    zu = jax.nn.gelu(pu_ref[...])
    v = _layer_norm(jax.nn.gelu(pv_ref[...]), slng_ref[...], slnb_ref[...]).astype(BF16)
    for h in range(SGU_HEADS):
        cols = slice(h * SGU_CHUNK, (h + 1) * SGU_CHUNK)
        mixed = _dot(sguw_ref[h], v[:, cols]) + sgub_ref[:, h:h + 1]
        cat_ref[:, POOL_WIDTH + h * SGU_CHUNK:POOL_WIDTH + (h + 1) * SGU_CHUNK] = (zu[:, cols] * mixed).astype(BF16)

    y = _dot(cat_ref[...], wout_ref[...])
    o_ref[...] = _residual_ln(x_ref[...], y, m_ref[0], 2, g_ref[...], b_ref[...])


def _even_mix(p, x, mod, poolw_bf, pscale, slng, slnb, sguw_bd, sgub_t, wout_bf, g, b, nlat):
    n = x.shape[0]
    ntiles = n // T
    hb = T // HALO
    nhb = n // HALO
    kern = functools.partial(_even_mix_kernel, nlat=nlat, ntiles=ntiles)
    const2 = lambda i: (0, 0)
    const3 = lambda i: (0, 0, 0)
    return pl.pallas_call(
        kern,
        grid=(ntiles,),
        in_specs=[
            pl.BlockSpec((T, POOL_WIDTH), lambda i: (i, 0)),
            pl.BlockSpec((HALO, POOL_WIDTH), lambda i: (jnp.maximum(i * hb - 1, 0), 0)),
            pl.BlockSpec((HALO, POOL_WIDTH), lambda i: (jnp.minimum((i + 1) * hb, nhb - 1), 0)),
            pl.BlockSpec((T, SGU_WIDTH), lambda i: (i, 1)),
            pl.BlockSpec((T, SGU_WIDTH), lambda i: (i, 2)),
            pl.BlockSpec((T, D), lambda i: (i, 0)),
            pl.BlockSpec((1, MOD_ROWS, D), lambda i: (i // nlat, 0, 0)),
            pl.BlockSpec((POOL_GROUPS, POOL_GROUP_DIM, POOL_GROUP_DIM), const3),
            pl.BlockSpec((POOL_GROUPS, POOL_GROUP_DIM), const2),
            pl.BlockSpec((1, SGU_WIDTH), const2),
            pl.BlockSpec((1, SGU_WIDTH), const2),
            pl.BlockSpec((SGU_HEADS, T, T), const3),
            pl.BlockSpec((T, SGU_HEADS), const2),
            pl.BlockSpec((D, D), const2),
            pl.BlockSpec((1, D), const2),
            pl.BlockSpec((1, D), const2),
        ],
        out_specs=pl.BlockSpec((T, D), lambda i: (i, 0)),
        out_shape=jax.ShapeDtypeStruct((n, D), F32),
        scratch_shapes=[pltpu.VMEM((T + 2 * HALO, POOL_WIDTH), F32), pltpu.VMEM((T, D), BF16)],
        compiler_params=_cparams(("arbitrary",)),
        name="even_mix",
    )(p, p, p, p, p, x, mod, poolw_bf, pscale, slng, slnb, sguw_bd, sgub_t, wout_bf, g, b)


def _rope_kernel(pq_ref, pk_ref, pv_ref, cos_ref, sin_ref, q_ref, k_ref, v_ref):
    cos = cos_ref[...]
    sin = sin_ref[...]
    lane = lax.broadcasted_iota(I32, (1, DIFF_QK_DIM), 1)
    low_half = (lane % 64) < 32
    q_scale = DIFF_QK_DIM ** -0.5 * LOG2E

    def rope(x):
        rot = jnp.where(low_half, -pltpu.roll(x, 96, 1), pltpu.roll(x, 32, 1))
        return x * cos + rot * sin

    for j in range(DIFF_WIDTH // DIFF_QK_DIM):
        cols = slice(j * DIFF_QK_DIM, (j + 1) * DIFF_QK_DIM)
        q_ref[:, cols] = (rope(pq_ref[:, cols]) * q_scale).astype(BF16)
        k_ref[:, cols] = rope(pk_ref[:, cols]).astype(BF16)
    v_ref[...] = pv_ref[...].astype(BF16)


def _rope_prep(p, cos, sin):
    n = p.shape[0]
    blk = lambda c: pl.BlockSpec((T, DIFF_WIDTH), lambda i: (i, c))
    tab = pl.BlockSpec((T, DIFF_QK_DIM), lambda i: (i, 0))
    shp = jax.ShapeDtypeStruct((n, DIFF_WIDTH), BF16)
    return pl.pallas_call(
        _rope_kernel,
        grid=(n // T,),
        in_specs=[blk(0), blk(1), blk(2), tab, tab],
        out_specs=[blk(0), blk(0), blk(0)],
        out_shape=[shp, shp, shp],
        compiler_params=_cparams(("arbitrary",)),
        name="rope_prep",
    )(p, p, p, cos, sin)


def _attn_kernel(q_ref, k_ref, v_ref, dl_ref, sub_ref, o_ref, acc_ref, m_ref, l_ref, a_ref, s_ref, p_ref,
                 *, n_lat_chunks, ctx_start, n_ctx_keys, lam_init):
    m_ref[...] = jnp.full(m_ref.shape, -jnp.inf, F32)
    l_ref[...] = jnp.zeros(l_ref.shape, F32)
    acc_ref[...] = jnp.zeros(acc_ref.shape, F32)
    tq = q_ref.shape[0]

    def scores(start, size, b):
        for c in range(2):
            cols = slice(c * DIFF_QK_DIM, (c + 1) * DIFF_QK_DIM)
            s_ref[b, c, :, 0:size] = _dot_nt(q_ref[:, cols], k_ref[pl.ds(start, size), cols])

    def softmax(size, b):
        for c in range(2):
            for r in range(tq // ATT_RB):
                rows = slice(r * ATT_RB, (r + 1) * ATT_RB)
                s = s_ref[b, c, rows, 0:size]
                m_old = m_ref[c, rows, :]
                m_new = jnp.maximum(m_old, jnp.max(s, axis=-1, keepdims=True))
                alpha = jnp.exp2(m_old - m_new)
                lsum = None
                for j in range(size // LANES):
                    pj = jnp.exp2(s[:, j * LANES:(j + 1) * LANES] - m_new)
                    p_ref[b, c, rows, j * LANES:(j + 1) * LANES] = pj.astype(BF16)
                    lsum = pj if lsum is None else lsum + pj
                l_ref[c, rows, :] = alpha * l_ref[c, rows, :] + lsum
                m_ref[c, rows, :] = m_new
                a_ref[b, c, rows, :] = alpha

    def values(start, size, b):
        vv = v_ref[pl.ds(start, size), :]
        for c in range(2):
            alpha = a_ref[b, c]
            pv = _dot(p_ref[b, c, :, 0:size], vv)
            acc_ref[c] = jnp.concatenate([alpha, alpha], axis=1) * acc_ref[c] + pv

    n_chunks = n_lat_chunks + 1

    def chunk(j):
        if isinstance(j, int) and j == n_lat_chunks:
            return ctx_start, n_ctx_keys
        return (j * ATT_TK if isinstance(j, int) else pl.multiple_of(j * ATT_TK, ATT_TK)), ATT_TK

    def step(t, b):
        if not isinstance(t, int) or 0 <= t - 2 < n_chunks:
            values(*chunk(t - 2), (b - 2) % ATT_NBUF)
        if not isinstance(t, int) or t < n_chunks:
            scores(*chunk(t), b)
        if not isinstance(t, int) or 0 <= t - 1 < n_chunks:
            softmax(chunk(t - 1)[1] if isinstance(t, int) else ATT_TK, (b - 1) % ATT_NBUF)

    first_steady, n_steady = 2, max(n_lat_chunks - 2, 0)
    trips = n_steady // ATT_NBUF
    for t in range(first_steady):
        step(t, t % ATT_NBUF)

    def trip(i, carry):
        for u in range(ATT_NBUF):
            step(first_steady + ATT_NBUF * i + u, (first_steady + u) % ATT_NBUF)
        return carry

    lax.fori_loop(0, trips, trip, 0)
    for t in range(first_steady + trips * ATT_NBUF, n_chunks + 2):
        step(t, t % ATT_NBUF)

    dl = dl_ref[...]
    lam = (jnp.exp(jnp.sum(dl[0:1] * dl[1:2], axis=-1, keepdims=True))
           - jnp.exp(jnp.sum(dl[2:3] * dl[3:4], axis=-1, keepdims=True)) + lam_init)
    l0 = jnp.sum(l_ref[0], axis=-1, keepdims=True)
    l1 = jnp.sum(l_ref[1], axis=-1, keepdims=True)
    o = acc_ref[0] / l0 - lam * (acc_ref[1] / l1)
    o = o * lax.rsqrt(jnp.mean(o * o, axis=-1, keepdims=True) + RMS_EPS) * sub_ref[...]
    o_ref[...] = (o * (1.0 - lam_init)).astype(BF16)


def _attn_call(q, k, v, dl, subln, lam_init, *, tq, q_tile0, n_q_tiles, n_lat_chunks, ctx_start, n_ctx_keys):
    n = k.shape[0]
    kern = functools.partial(_attn_kernel, n_lat_chunks=n_lat_chunks, ctx_start=ctx_start,
                             n_ctx_keys=n_ctx_keys, lam_init=lam_init)
    resident = lambda: pl.BlockSpec((n, DIFF_V_DIM), lambda h, i: (0, h), pipeline_mode=pl.Buffered(1))
    return pl.pallas_call(
        kern,
        grid=(DIFF_HEADS, n_q_tiles),
        in_specs=[
            pl.BlockSpec((tq, DIFF_V_DIM), lambda h, i: (q_tile0 + i, h)),
            resident(),
            resident(),
            pl.BlockSpec((4, DIFF_QK_DIM), lambda h, i: (0, 0)),
            pl.BlockSpec((1, DIFF_V_DIM), lambda h, i: (0, 0)),
        ],
        out_specs=pl.BlockSpec((tq, DIFF_V_DIM), lambda h, i: (i, h)),
        out_shape=jax.ShapeDtypeStruct((n_q_tiles * tq, DIFF_WIDTH), BF16),
        scratch_shapes=[pltpu.VMEM((2, tq, DIFF_V_DIM), F32), pltpu.VMEM((2, tq, LANES), F32),
                        pltpu.VMEM((2, tq, LANES), F32), pltpu.VMEM((ATT_NBUF, 2, tq, LANES), F32),
                        pltpu.VMEM((ATT_NBUF, 2, tq, ATT_TK), F32),
                        pltpu.VMEM((ATT_NBUF, 2, tq, ATT_TK), BF16)],
        compiler_params=_cparams(("arbitrary", "arbitrary")),
        name="diff_attention",
    )(q, k, v, dl, subln)


def _diff_attention(q, k, v, dl, subln, n_lat, lam_init):
    n = q.shape[0]
    n_ctx = n - n_lat
    o_lat = _attn_call(q, k, v, dl, subln, lam_init, tq=ATT_TQ, q_tile0=0, n_q_tiles=n_lat // ATT_TQ,
                       n_lat_chunks=n_lat // ATT_TK, ctx_start=n_lat, n_ctx_keys=n_ctx)
    o_ctx = _attn_call(q, k, v, dl, subln, lam_init, tq=n_ctx, q_tile0=n_lat // n_ctx, n_q_tiles=1,
                       n_lat_chunks=0, ctx_start=n_lat, n_ctx_keys=n_ctx)
    return jnp.concatenate([o_lat, o_ctx], axis=0)


def _odd_out_kernel(on_ref, xin_ref, gb_ref, gc_ref, xinp_ref, gcp_ref, xinn_ref, gcn_ref,
                    cw_ref, x_ref, m_ref, wout_ref, g_ref, b_ref, o_ref, e_ref, cat_ref,
                    *, nlat, ntiles):
    i = pl.program_id(0)
    first, last, _, _ = _seq_flags(i, nlat, ntiles)
    u = gc_ref[...] * xin_ref[...]
    e_ref[0:HALO, :] = jnp.where(first, 0.0, gcp_ref[...] * xinp_ref[...])
    e_ref[HALO:HALO + T, :] = u
    e_ref[HALO + T:HALO + T + HALO, :] = jnp.where(last, 0.0, gcn_ref[...] * xinn_ref[...])
    cw = cw_ref[...]
    z = (cw[0:1, :] * e_ref[HALO - 1:HALO - 1 + T, :] + cw[1:2, :] * u
         + cw[2:3, :] * e_ref[HALO + 1:HALO + 1 + T, :])
    cat_ref[:, 0:DIFF_WIDTH] = on_ref[...]
    cat_ref[:, DIFF_WIDTH:D] = (gb_ref[...] * z).astype(BF16)
    y = _dot(cat_ref[...], wout_ref[...])
    o_ref[...] = _residual_ln(x_ref[...], y, m_ref[0], 2, g_ref[...], b_ref[...])


def _odd_out(on, p, conv_w, x, mod, wout_bf, g, b, nlat):
    n = x.shape[0]
    ntiles = n // T
    hb = T // HALO
    nhb = n // HALO
    kern = functools.partial(_odd_out_kernel, nlat=nlat, ntiles=ntiles)
    const2 = lambda i: (0, 0)
    blk = lambda c: pl.BlockSpec((T, CONV_WIDTH), lambda i: (i, c))
    prev = lambda c: pl.BlockSpec((HALO, CONV_WIDTH), lambda i: (jnp.maximum(i * hb - 1, 0), c))
    nxt = lambda c: pl.BlockSpec((HALO, CONV_WIDTH), lambda i: (jnp.minimum((i + 1) * hb, nhb - 1), c))
    return pl.pallas_call(
        kern,
        grid=(ntiles,),
        in_specs=[
            pl.BlockSpec((T, DIFF_WIDTH), lambda i: (i, 0)),
            blk(3), blk(4), blk(5), prev(3), prev(5), nxt(3), nxt(5),
            pl.BlockSpec((3, CONV_WIDTH), const2),
            pl.BlockSpec((T, D), lambda i: (i, 0)),
            pl.BlockSpec((1, MOD_ROWS, D), lambda i: (i // nlat, 0, 0)),
            pl.BlockSpec((D, D), const2),
            pl.BlockSpec((1, D), const2),
            pl.BlockSpec((1, D), const2),
        ],
        out_specs=pl.BlockSpec((T, D), lambda i: (i, 0)),
        out_shape=jax.ShapeDtypeStruct((n, D), F32),
        scratch_shapes=[pltpu.VMEM((T + 2 * HALO, CONV_WIDTH), F32), pltpu.VMEM((T, D), BF16)],
        compiler_params=_cparams(("arbitrary",)),
        name="odd_out",
    )(on, p, p, p, p, p, p, p, conv_w, x, mod, wout_bf, g, b)


def _router_kernel(x_ref, m_ref, rw_ref, rb_ref, eidx_ref, gate_ref, rank_ref, cnt_ref, base_ref):
    i = pl.program_id(0)

    @pl.when(i == 0)
    def _():
        base_ref[...] = jnp.zeros(base_ref.shape, F32)

    m = m_ref[0]
    h = x_ref[...] * (1.0 + m[4:5, :]) + m[3:4, :]
    h_hi = h.astype(BF16)
    h_lo = (h - h_hi.astype(F32)).astype(BF16)
    rw = rw_ref[...]
    rw_hi = rw.astype(BF16)
    rw_lo = (rw - rw_hi.astype(F32)).astype(BF16)
    logits = _dot_nt(rw_hi, h_hi) + (_dot_nt(rw_hi, h_lo) + _dot_nt(rw_lo, h_hi))
    scores = _sigmoid(logits)
    biased = scores + rb_ref[...]

    neg = -jnp.inf
    b3 = biased.reshape(N_GROUPS, GROUP_SIZE, T)
    io3 = lax.broadcasted_iota(I32, b3.shape, 1)
    m1 = jnp.max(b3, axis=1, keepdims=True)
    f1 = jnp.min(jnp.where(b3 == m1, io3, GROUP_SIZE), axis=1, keepdims=True)
    m2 = jnp.max(jnp.where(io3 == f1, neg, b3), axis=1, keepdims=True)
    gs = (m1 + m2).reshape(N_GROUPS, T)

    gio = lax.broadcasted_iota(I32, gs.shape, 0)
    gsel = jnp.zeros(gs.shape, F32)
    for _ in range(TOPK_GROUPS):
        mx = jnp.max(gs, axis=0, keepdims=True)
        f = jnp.min(jnp.where(gs == mx, gio, N_GROUPS), axis=0, keepdims=True)
        hit = gio == f
        gsel = jnp.where(hit, 1.0, gsel)
        gs = jnp.where(hit, neg, gs)
    masked = jnp.where(gsel.reshape(N_GROUPS, 1, T) > 0.5, b3, neg).reshape(N_EXPERTS, T)

    eio = lax.broadcasted_iota(I32, masked.shape, 0)
    hits, gates, eids = [], [], []
    onehot = jnp.zeros(masked.shape, F32)
    for _ in range(TOP_K):
        mx = jnp.max(masked, axis=0, keepdims=True)
        f = jnp.min(jnp.where(masked == mx, eio, N_EXPERTS), axis=0, keepdims=True)
        hit = eio == f
        hits.append(hit)
        eids.append(f)
        gates.append(jnp.sum(jnp.where(hit, scores, 0.0), axis=0, keepdims=True))
        onehot = jnp.where(hit, 1.0, onehot)
        masked = jnp.where(hit, neg, masked)
    gsum = gates[0]
    for gk in gates[1:]:
        gsum = gsum + gk

    r_io = lax.broadcasted_iota(I32, (T, T), 0)
    c_io = lax.broadcasted_iota(I32, (T, T), 1)
    upper = jnp.where(r_io < c_io, 1.0, 0.0).astype(BF16)
    before = base_ref[...] + _dot(onehot.astype(BF16), upper)
    for k in range(TOP_K):
        eidx_ref[k:k + 1, :] = eids[k]
        gate_ref[k:k + 1, :] = gates[k] / gsum * ROUTED_SCALE
        rank_ref[k:k + 1, :] = jnp.sum(jnp.where(hits[k], before, 0.0), axis=0, keepdims=True).astype(I32)
    base_new = base_ref[...] + jnp.sum(onehot, axis=1, keepdims=True)
    base_ref[...] = base_new
    cnt_ref[...] = jnp.broadcast_to(base_new, cnt_ref.shape)


def _router(x, mod, rw_t, rb_col, nlat):
    n = x.shape[0]
    kt = pl.BlockSpec((TOP_K, T), lambda i: (0, i))
    return pl.pallas_call(
        _router_kernel,
        grid=(n // T,),
        in_specs=[
            pl.BlockSpec((T, D), lambda i: (i, 0)),
            pl.BlockSpec((1, MOD_ROWS, D), lambda i: (i // nlat, 0, 0)),
            pl.BlockSpec((N_EXPERTS, D), lambda i: (0, 0)),
            pl.BlockSpec((N_EXPERTS, 1), lambda i: (0, 0)),
        ],
        out_specs=[kt, kt, kt, pl.BlockSpec((N_EXPERTS, 128), lambda i: (0, 0))],
        out_shape=[jax.ShapeDtypeStruct((TOP_K, n), I32), jax.ShapeDtypeStruct((TOP_K, n), F32),
                   jax.ShapeDtypeStruct((TOP_K, n), I32), jax.ShapeDtypeStruct((N_EXPERTS, 128), F32)],
        scratch_shapes=[pltpu.VMEM((N_EXPERTS, 1), F32)],
        compiler_params=_cparams(("arbitrary",)),
        name="moe_router",
    )(x, mod, rw_t, rb_col)


def _row_copy(src, src_row, dst, dst_row, sem):
    return pltpu.make_async_copy(src.at[pl.ds(src_row, 1), :], dst.at[pl.ds(dst_row, 1), :], sem)


def _pack_bf16_pair(v):
    half = v.shape[1] // 2
    lo = pltpu.bitcast(v[:, :half].astype(BF16).astype(F32), U32)
    hi = pltpu.bitcast(v[:, half:].astype(BF16).astype(F32), U32)
    return hi | (lo >> 16)


def _unpack_bf16_pair(w):
    lo = pltpu.bitcast(w << 16, F32)
    hi = pltpu.bitcast(w & jnp.uint32(0xFFFF0000), F32)
    return lo, hi


def _dispatch_kernel(dest_ref, x_ref, m_ref, xs_ref, h_ref, sem):
    i = pl.program_id(0)
    slot = i % 2
    m = m_ref[0]
    h_ref[slot] = _pack_bf16_pair(x_ref[...] * (1.0 + m[4:5, :]) + m[3:4, :])

    def body(tb, carry):
        for u in range(DMA_UNROLL):
            t = tb * DMA_UNROLL + u
            for k in range(TOP_K):
                _row_copy(h_ref.at[slot], t, xs_ref, dest_ref[0, k, t], sem.at[slot]).start(priority=k % 2)
        return carry

    lax.fori_loop(0, T // DMA_UNROLL, body, 0)

    def drain(s):
        for _ in range(TOP_K):
            pltpu.make_async_copy(h_ref.at[s], xs_ref.at[pl.ds(0, T), :], sem.at[s]).wait()

    @pl.when(i > 0)
    def _():
        drain(1 - slot)

    @pl.when(i == pl.num_programs(0) - 1)
    def _():
        drain(slot)


def _dispatch(dest3, x, mod, nlat):
    n = x.shape[0]
    return pl.pallas_call(
        _dispatch_kernel,
        grid=(n // T,),
        in_specs=[
            pl.BlockSpec((1, TOP_K, T), lambda i: (i, 0, 0), memory_space=pltpu.SMEM),
            pl.BlockSpec((T, D), lambda i: (i, 0)),
            pl.BlockSpec((1, MOD_ROWS, D), lambda i: (i // nlat, 0, 0)),
        ],
        out_specs=pl.BlockSpec(memory_space=pl.ANY),
        out_shape=jax.ShapeDtypeStruct((n * TOP_K, D // 2), U32),
        scratch_shapes=[pltpu.VMEM((2, T, D // 2), U32), pltpu.SemaphoreType.DMA((2,))],
        compiler_params=_cparams(("arbitrary",)),
        name="moe_dispatch",
    )(dest3, x, mod)


def _expert_kernel(vt_ref, ve_ref, vlo_ref, vhi_ref, vnew_ref, nv_ref, xs_ref, wg_ref, wu_ref, wd_ref,
                   ys_ref, wgu_bf, wd_bf):
    v = pl.program_id(0)

    @pl.when(v < nv_ref[0])
    def _():
        @pl.when(vnew_ref[v] == 1)
        def _():
            wgu_bf[:, 0:EXPERT_FF] = wg_ref[0, 0].astype(BF16)
            wgu_bf[:, EXPERT_FF:2 * EXPERT_FF] = wu_ref[0, 0].astype(BF16)
            wd_bf[...] = wd_ref[0, 0].astype(BF16)

        def ffn(r):
            x_lo, x_hi = _unpack_bf16_pair(xs_ref[r * EXPERT_RB:(r + 1) * EXPERT_RB, :])
            xb = jnp.concatenate([x_lo.astype(BF16), x_hi.astype(BF16)], axis=1)
            h12 = _dot(xb, wgu_bf[...])
            a = (_silu(h12[:, 0:EXPERT_FF]) * h12[:, EXPERT_FF:2 * EXPERT_FF]).astype(BF16)
            return _pack_bf16_pair(_dot(a, wd_bf[...]))

        lo = vlo_ref[v]
        hi = vhi_ref[v]
        whole = jnp.logical_and(lo == 0, hi == TS)

        @pl.when(whole)
        def _():
            for r in range(TS // EXPERT_RB):
                ys_ref[r * EXPERT_RB:(r + 1) * EXPERT_RB, :] = ffn(r)

        @pl.when(jnp.logical_not(whole))
        def _():
            for r in range(TS // EXPERT_RB):
                c0, c1 = r * EXPERT_RB, (r + 1) * EXPERT_RB
                blk = slice(c0, c1)

                @pl.when(jnp.logical_and(lo < c1, hi > c0))
                def _():
                    y = ffn(r)
                    row = c0 + lax.broadcasted_iota(I32, (EXPERT_RB, 1), 0)
                    mine = jnp.logical_and(row >= lo, row < hi)
                    starts_before = lo <= c0

                    @pl.when(jnp.logical_and(starts_before, hi >= c1))
                    def _():
                        ys_ref[blk, :] = y

                    @pl.when(jnp.logical_and(starts_before, hi < c1))
                    def _():
                        ys_ref[blk, :] = jnp.where(mine, y, jnp.uint32(0))

                    @pl.when(jnp.logical_not(starts_before))
                    def _():
                        ys_ref[blk, :] = jnp.where(mine, y, ys_ref[blk, :])


def _expert_ffn(sched, xs, w_gate, w_up, w_down, li):
    vt, ve, vlo, vhi, vnew, nv = sched
    nslots = xs.shape[0]
    nvis = vt.shape[0]
    w_in = lambda v, vt, ve, *_: (li, ve[v], 0, 0)
    rows = lambda v, vt, *_: (vt[v], 0)
    grid_spec = pltpu.PrefetchScalarGridSpec(
        num_scalar_prefetch=6,
        grid=(nvis,),
        in_specs=[
            pl.BlockSpec((TS, D // 2), rows),
            pl.BlockSpec((1, 1, D, EXPERT_FF), w_in),
            pl.BlockSpec((1, 1, D, EXPERT_FF), w_in),
            pl.BlockSpec((1, 1, EXPERT_FF, D), w_in),
        ],
        out_specs=pl.BlockSpec((TS, D // 2), rows),
        scratch_shapes=[pltpu.VMEM((D, 2 * EXPERT_FF), BF16), pltpu.VMEM((EXPERT_FF, D), BF16)],
    )
    return pl.pallas_call(
        _expert_kernel,
        grid_spec=grid_spec,
        out_shape=jax.ShapeDtypeStruct((nslots, D // 2), U32),
        compiler_params=_cparams(("arbitrary",)),
        name="moe_experts",
    )(vt, ve, vlo, vhi, vnew, nv, xs, w_gate, w_up, w_down)


def _combine_kernel(dest_ref, ys_ref, x_ref, m_ref, gt_ref, sg_ref, su_ref, sd_ref, g_ref, b_ref,
                    o_ref, *scratch):
    bufs, sem = scratch[:TOP_K], scratch[TOP_K]

    def body(tb, carry):
        for u in range(DMA_UNROLL):
            t = tb * DMA_UNROLL + u
            for k in range(TOP_K):
                _row_copy(ys_ref, dest_ref[0, k, t], bufs[k], t, sem).start(priority=k % 2)
        return carry

    lax.fori_loop(0, T // DMA_UNROLL, body, 0)

    m = m_ref[0]
    x = x_ref[...]
    hb = (x * (1.0 + m[4:5, :]) + m[3:4, :]).astype(BF16)
    f = _dot((_silu(_dot(hb, sg_ref[...])) * _dot(hb, su_ref[...])).astype(BF16), sd_ref[...])

    for k in range(TOP_K):
        pltpu.make_async_copy(ys_ref.at[pl.ds(0, T), :], bufs[k], sem).wait()
    gt = gt_ref[...]
    f_lo = f[:, :D // 2]
    f_hi = f[:, D // 2:]
    for k in range(TOP_K):
        y_lo, y_hi = _unpack_bf16_pair(bufs[k][...])
        f_lo = f_lo + gt[:, k:k + 1] * y_lo
        f_hi = f_hi + gt[:, k:k + 1] * y_hi
    f = jnp.concatenate([f_lo, f_hi], axis=1)
    o_ref[...] = _residual_ln(x, f, m, 5, g_ref[...], b_ref[...])


def _combine(dest3, ys, x, mod, gates_t, sg_bf, su_bf, sd_bf, g, b, nlat):
    n = x.shape[0]
    const2 = lambda i: (0, 0)
    return pl.pallas_call(
        _combine_kernel,
        grid=(n // T,),
        in_specs=[
            pl.BlockSpec((1, TOP_K, T), lambda i: (i, 0, 0), memory_space=pltpu.SMEM),
            pl.BlockSpec(memory_space=pl.ANY),
            pl.BlockSpec((T, D), lambda i: (i, 0)),
            pl.BlockSpec((1, MOD_ROWS, D), lambda i: (i // nlat, 0, 0)),
            pl.BlockSpec((T, TOP_K), lambda i: (i, 0)),
            pl.BlockSpec((D, EXPERT_FF), const2),
            pl.BlockSpec((D, EXPERT_FF), const2),
            pl.BlockSpec((EXPERT_FF, D), const2),
            pl.BlockSpec((1, D), const2),
            pl.BlockSpec((1, D), const2),
        ],
        out_specs=pl.BlockSpec((T, D), lambda i: (i, 0)),
        out_shape=jax.ShapeDtypeStruct((n, D), F32),
        scratch_shapes=[pltpu.VMEM((T, D // 2), U32)] * TOP_K + [pltpu.SemaphoreType.DMA],
        compiler_params=_cparams(("arbitrary",)),
        name="moe_combine",
    )(dest3, ys, x, mod, gates_t, sg_bf, su_bf, sd_bf, g, b)


def _slot_schedule(counts, n_slot_tiles):
    offs = jnp.cumsum(counts) - counts
    ends = offs + counts
    first_tile = offs // TS
    last_tile = jnp.maximum(ends - 1, offs) // TS
    nvis_e = jnp.where(counts > 0, last_tile - first_tile + 1, 0)
    vstart = jnp.cumsum(nvis_e) - nvis_e
    total = jnp.sum(nvis_e)
    nvis = n_slot_tiles + N_EXPERTS - 1
    v = jnp.arange(nvis, dtype=I32)
    experts = jnp.arange(N_EXPERTS, dtype=I32)
    e = jnp.minimum(jnp.sum((vstart + nvis_e)[None, :] <= v[:, None], axis=1), N_EXPERTS - 1).astype(I32)
    onehot = e[:, None] == experts[None, :]
    pick = lambda a: jnp.sum(jnp.where(onehot, a[None, :], 0), axis=1)
    tile = pick(first_tile) + (v - pick(vstart))
    lo = jnp.maximum(pick(offs), tile * TS) - tile * TS
    hi = jnp.minimum(pick(ends), (tile + 1) * TS) - tile * TS
    valid = v < total
    e_last = jnp.max(jnp.where(counts > 0, experts, 0))
    vt = jnp.where(valid, tile, n_slot_tiles - 1).astype(I32)
    ve = jnp.where(valid, e, e_last).astype(I32)
    vlo = jnp.where(valid, lo, 0).astype(I32)
    vhi = jnp.where(valid, hi, 0).astype(I32)
    vnew = jnp.concatenate([jnp.ones((1,), I32), (ve[1:] != ve[:-1]).astype(I32)])
    return offs, (vt, ve, vlo, vhi, vnew, total.astype(I32).reshape(1))


def _moe_layer(x, mod, rw_t, rb_col, w_gate, w_up, w_down, li, sg_bf, su_bf, sd_bf, g, b, nlat):
    n = x.shape[0]
    eidx, gates, rank, cnt = _router(x, mod, rw_t, rb_col, nlat)
    counts = cnt[:, 0].astype(I32)
    offs, sched = _slot_schedule(counts, n * TOP_K // TS)
    experts = jnp.arange(N_EXPERTS, dtype=I32)
    dest = rank + jnp.sum(jnp.where(eidx[..., None] == experts, offs, 0), axis=-1)
    dest3 = dest.reshape(TOP_K, n // T, T).transpose(1, 0, 2)
    xs = _dispatch(dest3, x, mod, nlat)
    ys = _expert_ffn(sched, xs, w_gate, w_up, w_down, li)
    return _combine(dest3, ys, x, mod, gates.T, sg_bf, su_bf, sd_bf, g, b, nlat)


def _rope_tables(n_lat, n_ctx):
    rows = n_lat // GRID_W
    r, cidx = jnp.meshgrid(jnp.arange(rows), jnp.arange(GRID_W), indexing="ij")
    pos = jnp.stack([r.reshape(-1), cidx.reshape(-1)], axis=-1).astype(F32)
    nf = DIFF_QK_DIM // 4
    inv = ROPE_BASE ** (-jnp.arange(nf, dtype=F32) / nf)
    ang = jnp.broadcast_to(pos[:, :, None, None] * inv, (n_lat, 2, 2, nf)).reshape(n_lat, DIFF_QK_DIM)
    cos = jnp.concatenate([jnp.cos(ang), jnp.ones((n_ctx, DIFF_QK_DIM), F32)], axis=0)
    sin = jnp.concatenate([jnp.sin(ang), jnp.zeros((n_ctx, DIFF_QK_DIM), F32)], axis=0)
    return cos, sin


def kernel(x, c, ctx, c_ctx, ada_w, ada_b, ln_g, ln_b, ev_w_in, ev_w_out, pool_w, pool_scale, sgu_ln_g, sgu_ln_b, sgu_w, sgu_b, od_w_in, od_w_out, diff_lambda, diff_subln, conv_w, router_w, router_bias, exp_w_gate, exp_w_up, exp_w_down, sh_w_gate, sh_w_up, sh_w_down):
    bsz, n_lat, d = x.shape
    n_ctx = ctx.shape[1]
    assert bsz == 1 and d == D and n_lat % T == 0 and n_ctx % T == 0
    assert n_lat % ATT_TK == 0 and n_lat % ATT_TQ == 0 and n_lat % n_ctx == 0 and n_ctx % LANES == 0
    depth = ada_w.shape[0]
    nlat = n_lat // T

    xs = jnp.concatenate([x[0], ctx[0]], axis=0)
    mods = _ada_mods(jnp.stack([c[0], c_ctx], axis=1), ada_w, ada_b)
    cos, sin = _rope_tables(n_lat, n_ctx)
    eye = jnp.eye(T // SGU_CHUNK, dtype=F32)

    for li in range(depth):
        j = li // 2
        mod = mods[li]
        g0, b0 = ln_g[li, 0][None], ln_b[li, 0][None]
        g1, b1 = ln_g[li, 1][None], ln_b[li, 1][None]
        if li % 2 == 0:
            p = _mod_matmul(xs, mod, ev_w_in[j].astype(BF16), nlat, sh=0, sc=1, tn=EVEN_IN)
            sguw_bd = jnp.einsum("ab,hpq->hapbq", eye, sgu_w[j]).reshape(SGU_HEADS, T, T).astype(BF16)
            sgub_t = jnp.tile(sgu_b[j].T, (T // SGU_CHUNK, 1))
            xs = _even_mix(p, xs, mod, pool_w[j].astype(BF16), pool_scale[j], sgu_ln_g[j][None], sgu_ln_b[j][None],
                           sguw_bd, sgub_t, ev_w_out[j].astype(BF16), g0, b0, nlat)
        else:
            p = _mod_matmul(xs, mod, od_w_in[j].astype(BF16), nlat, sh=0, sc=1, tn=ODD_IN // 2)
            q, k, v = _rope_prep(p, cos, sin)
            lam_init = 0.8 - 0.6 * math.exp(-0.3 * li)
            on = _diff_attention(q, k, v, diff_lambda[j], diff_subln[j][None], n_lat, lam_init)
            xs = _odd_out(on, p, conv_w[j], xs, mod, od_w_out[j].astype(BF16), g0, b0, nlat)
        xs = _moe_layer(xs, mod, router_w[li].T, router_bias[li][:, None], exp_w_gate, exp_w_up,
                        exp_w_down, li, sh_w_gate[li].astype(BF16), sh_w_up[li].astype(BF16),
                        sh_w_down[li].astype(BF16), g1, b1, nlat)
    return xs[:n_lat][None]
```

```python
import functools
import math

import jax
import jax.numpy as jnp
from jax import lax
from jax.experimental import pallas as pl
from jax.experimental.pallas import tpu as pltpu

F32 = jnp.float32
BF16 = jnp.bfloat16
I32 = jnp.int32
U32 = jnp.uint32

D = 2048
DEPTH = 4
GRID_W = 64

POOL_GROUPS = 4
POOL_WINDOWS = (2, 4, 8, 16)
POOL_WIDTH = 1024
POOL_GROUP_DIM = 256
SGU_WIDTH = 1024
SGU_CHUNK = 128
SGU_HEADS = 8
EVEN_IN = 3072

DIFF_HEADS = 4
DIFF_QK_DIM = 128
DIFF_V_DIM = 256
DIFF_WIDTH = 1024
CONV_WIDTH = 1024
ODD_IN = 6144
ROPE_BASE = 10000.0

N_EXPERTS = 64
EXPERT_FF = 384
TOP_K = 8
N_GROUPS = 8
GROUP_SIZE = 8
TOPK_GROUPS = 4
ROUTED_SCALE = 2.5

ALPHA = (2 * DEPTH) ** 0.25
LN_EPS = 1e-6
RMS_EPS = 1e-5

T = 256
TS = 512
EXPERT_RB = 128
DMA_UNROLL = 16
HALO = 8
MOD_ROWS = 8
ADA_TN = 1536
ADA_UNROLL = 8
ATT_TQ = 512
ATT_TK = 512
ATT_RB = 64
ATT_NBUF = 3
LANES = 128
LOG2E = 1.4426950408889634
VMEM_LIMIT = 56 * 1024 * 1024


def _cparams(sem, vmem=VMEM_LIMIT, flags=None):
    return pltpu.CompilerParams(dimension_semantics=sem, vmem_limit_bytes=vmem, flags=flags)


def _sigmoid(x):
    return 1.0 / (1.0 + jnp.exp(-x))


def _silu(x):
    return x * _sigmoid(x)


def _layer_norm(v, g, b):
    mu = jnp.mean(v, axis=-1, keepdims=True)
    d = v - mu
    var = jnp.mean(d * d, axis=-1, keepdims=True)
    return d * lax.rsqrt(var + LN_EPS) * g + b


def _dot(a, b):
    return jnp.dot(a, b, preferred_element_type=F32)


def _dot_nt(a, b):
    return lax.dot_general(a, b, (((1,), (1,)), ((), ())), preferred_element_type=F32)


def _ada_kernel(c_ref, w_ref, b_ref, o_ref, s_ref):
    tn = w_ref.shape[2]
    s_ref[...] = _silu(c_ref[...])

    def body(m, carry):
        a0, a1 = carry
        for u in range(ADA_UNROLL):
            r = pl.multiple_of(m * (8 * ADA_UNROLL) + 8 * u, 8)
            w = w_ref[0, pl.ds(r, 8), :]
            s = s_ref[pl.ds(r, 8), :]
            a0 = a0 + w * s[:, 0:1]
            a1 = a1 + w * s[:, 1:2]
        return a0, a1

    z = jnp.zeros((8, tn), F32)
    a0, a1 = lax.fori_loop(0, D // (8 * ADA_UNROLL), body, (z, z))
    bias = b_ref[0]
    o_ref[0] = jnp.zeros((8, tn), F32)
    o_ref[0, 0:1, :] = jnp.sum(a0, axis=0, keepdims=True) + bias
    o_ref[0, 1:2, :] = jnp.sum(a1, axis=0, keepdims=True) + bias


def _ada_mods(c_cols, ada_w, ada_b):
    depth = ada_w.shape[0]
    n6 = ada_w.shape[2]
    out = pl.pallas_call(
        _ada_kernel,
        grid=(depth, n6 // ADA_TN),
        in_specs=[
            pl.BlockSpec((D, 2), lambda l, j: (0, 0)),
            pl.BlockSpec((1, D, ADA_TN), lambda l, j: (l, 0, j)),
            pl.BlockSpec((1, 1, ADA_TN), lambda l, j: (l, 0, j)),
        ],
        out_specs=pl.BlockSpec((1, 8, ADA_TN), lambda l, j: (l, 0, j)),
        out_shape=jax.ShapeDtypeStruct((depth, 8, n6), F32),
        scratch_shapes=[pltpu.VMEM((D, 2), F32)],
        compiler_params=_cparams(("arbitrary", "arbitrary")),
        name="ada_mods",
    )(c_cols, ada_w, ada_b.reshape(depth, 1, n6))
    mods = out[:, :2, :].reshape(depth, 2, 6, D)
    return jnp.pad(mods, ((0, 0), (0, 0), (0, MOD_ROWS - 6), (0, 0)))


def _modmm_kernel(x_ref, m_ref, w_ref, o_ref, *, sh, sc):
    m = m_ref[0]
    h = x_ref[...] * (1.0 + m[sc:sc + 1, :]) + m[sh:sh + 1, :]
    o_ref[...] = _dot(h.astype(BF16), w_ref[...])


def _mod_matmul(x, mod, w_bf, nlat, *, sh, sc, tn):
    n = x.shape[0]
    nout = w_bf.shape[1]
    return pl.pallas_call(
        functools.partial(_modmm_kernel, sh=sh, sc=sc),
        grid=(nout // tn, n // T),
        in_specs=[
            pl.BlockSpec((T, D), lambda j, i: (i, 0)),
            pl.BlockSpec((1, MOD_ROWS, D), lambda j, i: (i // nlat, 0, 0)),
            pl.BlockSpec((D, tn), lambda j, i: (0, j)),
        ],
        out_specs=pl.BlockSpec((T, tn), lambda j, i: (i, j)),
        out_shape=jax.ShapeDtypeStruct((n, nout), F32),
        compiler_params=_cparams(("arbitrary", "arbitrary")),
        name="mod_matmul",
    )(x, mod, w_bf)


def _seq_flags(i, nlat, ntiles):
    is_ctx = i >= nlat
    first = jnp.logical_or(i == 0, i == nlat)
    last = jnp.logical_or(i == nlat - 1, i == ntiles - 1)
    t0 = jnp.where(is_ctx, i - nlat, i) * T
    nseq = jnp.where(is_ctx, (ntiles - nlat) * T, nlat * T)
    return first, last, t0, nseq


def _residual_ln(x, y, m, gate_row, g, b):
    return _layer_norm(ALPHA * x + m[gate_row:gate_row + 1, :] * y, g, b)


def _even_mix_kernel(pp_ref, pprev_ref, pnext_ref, pu_ref, pv_ref, x_ref, m_ref,
                     poolw_ref, pscale_ref, slng_ref, slnb_ref, sguw_ref, sgub_ref,
                     wout_ref, g_ref, b_ref, o_ref, e_ref, cat_ref, *, nlat, ntiles):
    i = pl.program_id(0)
    first, last, t0, nseq = _seq_flags(i, nlat, ntiles)

    e_ref[0:HALO, :] = jnp.where(first, 0.0, pprev_ref[...])
    e_ref[HALO:HALO + T, :] = pp_ref[...]
    e_ref[HALO + T:HALO + T + HALO, :] = jnp.where(last, 0.0, pnext_ref[...])

    pos = t0 + lax.broadcasted_iota(I32, (T, 1), 0)
    for g in range(POOL_GROUPS):
        w = POOL_WINDOWS[g]
        lo, hi = w // 2, w - 1 - w // 2
        cols = slice(g * POOL_GROUP_DIM, (g + 1) * POOL_GROUP_DIM)
        tot = e_ref[HALO - lo:HALO - lo + T, cols]
        for d in range(-lo + 1, hi + 1):
            tot = tot + e_ref[HALO + d:HALO + d + T, cols]
        cnt = jnp.minimum(pos + hi, nseq - 1) - jnp.maximum(pos - lo, 0) + 1
        pooled = tot / cnt.astype(F32) - e_ref[HALO:HALO + T, cols]
        a = _dot(pooled.astype(BF16), poolw_ref[g]) * pscale_ref[g:g + 1, :]
        cat_ref[:, cols] = a.astype(BF16)

    zu = jax.nn.gelu(pu_ref[...])
    v = _layer_norm(jax.nn.gelu(pv_ref[...]), slng_ref[...], slnb_ref[...]).astype(BF16)
    for h in range(SGU_HEADS):
        cols = slice(h * SGU_CHUNK, (h + 1) * SGU_CHUNK)
        mixed = _dot(sguw_ref[h], v[:, cols]) + sgub_ref[:, h:h + 1]
        cat_ref[:, POOL_WIDTH + h * SGU_CHUNK:POOL_WIDTH + (h + 1) * SGU_CHUNK] = (zu[:, cols] * mixed).astype(BF16)

    y = _dot(cat_ref[...], wout_ref[...])
    o_ref[...] = _residual_ln(x_ref[...], y, m_ref[0], 2, g_ref[...], b_ref[...])


def _even_mix(p, x, mod, poolw_bf, pscale, slng, slnb, sguw_bd, sgub_t, wout_bf, g, b, nlat):
    n = x.shape[0]
    ntiles = n // T
    hb = T // HALO
    nhb = n // HALO
    kern = functools.partial(_even_mix_kernel, nlat=nlat, ntiles=ntiles)
    const2 = lambda i: (0, 0)
    const3 = lambda i: (0, 0, 0)
    return pl.pallas_call(
        kern,
        grid=(ntiles,),
        in_specs=[
            pl.BlockSpec((T, POOL_WIDTH), lambda i: (i, 0)),
            pl.BlockSpec((HALO, POOL_WIDTH), lambda i: (jnp.maximum(i * hb - 1, 0), 0)),
            pl.BlockSpec((HALO, POOL_WIDTH), lambda i: (jnp.minimum((i + 1) * hb, nhb - 1), 0)),
            pl.BlockSpec((T, SGU_WIDTH), lambda i: (i, 1)),
            pl.BlockSpec((T, SGU_WIDTH), lambda i: (i, 2)),
            pl.BlockSpec((T, D), lambda i: (i, 0)),
            pl.BlockSpec((1, MOD_ROWS, D), lambda i: (i // nlat, 0, 0)),
            pl.BlockSpec((POOL_GROUPS, POOL_GROUP_DIM, POOL_GROUP_DIM), const3),
            pl.BlockSpec((POOL_GROUPS, POOL_GROUP_DIM), const2),
            pl.BlockSpec((1, SGU_WIDTH), const2),
            pl.BlockSpec((1, SGU_WIDTH), const2),
            pl.BlockSpec((SGU_HEADS, T, T), const3),
            pl.BlockSpec((T, SGU_HEADS), const2),
            pl.BlockSpec((D, D), const2),
            pl.BlockSpec((1, D), const2),
            pl.BlockSpec((1, D), const2),
        ],
        out_specs=pl.BlockSpec((T, D), lambda i: (i, 0)),
        out_shape=jax.ShapeDtypeStruct((n, D), F32),
        scratch_shapes=[pltpu.VMEM((T + 2 * HALO, POOL_WIDTH), F32), pltpu.VMEM((T, D), BF16)],
        compiler_params=_cparams(("arbitrary",)),
        name="even_mix",
    )(p, p, p, p, p, x, mod, poolw_bf, pscale, slng, slnb, sguw_bd, sgub_t, wout_bf, g, b)


def _qkv_kernel(x_ref, m_ref, w_ref, cos_ref, sin_ref, q_ref, k_ref, v_ref):
    m = m_ref[0]
    h = x_ref[...] * (1.0 + m[1:2, :]) + m[0:1, :]
    p = _dot(h.astype(BF16), w_ref[...])
    cos = cos_ref[...]
    sin = sin_ref[...]
    lane = lax.broadcasted_iota(I32, (1, DIFF_QK_DIM), 1)
    low_half = (lane % 64) < 32
    q_scale = DIFF_QK_DIM ** -0.5 * LOG2E

    def rope(x):
        rot = jnp.where(low_half, -pltpu.roll(x, 96, 1), pltpu.roll(x, 32, 1))
        return x * cos + rot * sin

    for j in range(DIFF_WIDTH // DIFF_QK_DIM):
        cols = slice(j * DIFF_QK_DIM, (j + 1) * DIFF_QK_DIM)
        kcols = slice(DIFF_WIDTH + j * DIFF_QK_DIM, DIFF_WIDTH + (j + 1) * DIFF_QK_DIM)
        q_ref[:, cols] = (rope(p[:, cols]) * q_scale).astype(BF16)
        k_ref[:, cols] = rope(p[:, kcols]).astype(BF16)
    v_ref[...] = p[:, 2 * DIFF_WIDTH:3 * DIFF_WIDTH].astype(BF16)


def _qkv_proj(x, mod, w_bf, cos, sin, nlat):
    n = x.shape[0]
    out = pl.BlockSpec((T, DIFF_WIDTH), lambda i: (i, 0))
    tab = pl.BlockSpec((T, DIFF_QK_DIM), lambda i: (i, 0))
    shp = jax.ShapeDtypeStruct((n, DIFF_WIDTH), BF16)
    return pl.pallas_call(
        _qkv_kernel,
        grid=(n // T,),
        in_specs=[
            pl.BlockSpec((T, D), lambda i: (i, 0)),
            pl.BlockSpec((1, MOD_ROWS, D), lambda i: (i // nlat, 0, 0)),
            pl.BlockSpec((D, 3 * DIFF_WIDTH), lambda i: (0, 0)),
            tab, tab,
        ],
        out_specs=[out, out, out],
        out_shape=[shp, shp, shp],
        compiler_params=_cparams(("arbitrary",)),
        name="qkv_proj",
    )(x, mod, w_bf, cos, sin)


def _attn_kernel(q_ref, k_ref, v_ref, dl_ref, sub_ref, o_ref, acc_ref, m_ref, l_ref, a_ref, s_ref, p_ref,
                 *, n_lat_chunks, ctx_start, n_ctx_keys, lam_init):
    m_ref[...] = jnp.full(m_ref.shape, -jnp.inf, F32)
    l_ref[...] = jnp.zeros(l_ref.shape, F32)
    acc_ref[...] = jnp.zeros(acc_ref.shape, F32)
    tq = q_ref.shape[0]

    def scores(start, size, b):
        for c in range(2):
            cols = slice(c * DIFF_QK_DIM, (c + 1) * DIFF_QK_DIM)
            s_ref[b, c, :, 0:size] = _dot_nt(q_ref[:, cols], k_ref[pl.ds(start, size), cols])

    def softmax(size, b):
        for c in range(2):
            for r in range(tq // ATT_RB):
                rows = slice(r * ATT_RB, (r + 1) * ATT_RB)
                s = s_ref[b, c, rows, 0:size]
                m_old = m_ref[c, rows, :]
                m_new = jnp.maximum(m_old, jnp.max(s, axis=-1, keepdims=True))
                alpha = jnp.exp2(m_old - m_new)
                lsum = None
                for j in range(size // LANES):
                    pj = jnp.exp2(s[:, j * LANES:(j + 1) * LANES] - m_new)
                    p_ref[b, c, rows, j * LANES:(j + 1) * LANES] = pj.astype(BF16)
                    lsum = pj if lsum is None else lsum + pj
                l_ref[c, rows, :] = alpha * l_ref[c, rows, :] + lsum
                m_ref[c, rows, :] = m_new
                a_ref[b, c, rows, :] = alpha

    def values(start, size, b):
        vv = v_ref[pl.ds(start, size), :]
        for c in range(2):
            alpha = a_ref[b, c]
            pv = _dot(p_ref[b, c, :, 0:size], vv)
            acc_ref[c] = jnp.concatenate([alpha, alpha], axis=1) * acc_ref[c] + pv

    n_chunks = n_lat_chunks + 1

    def chunk(j):
        if isinstance(j, int) and j == n_lat_chunks:
            return ctx_start, n_ctx_keys
        return (j * ATT_TK if isinstance(j, int) else pl.multiple_of(j * ATT_TK, ATT_TK)), ATT_TK

    def step(t, b):
        if not isinstance(t, int) or 0 <= t - 2 < n_chunks:
            values(*chunk(t - 2), (b - 2) % ATT_NBUF)
        if not isinstance(t, int) or t < n_chunks:
            scores(*chunk(t), b)
        if not isinstance(t, int) or 0 <= t - 1 < n_chunks:
            softmax(chunk(t - 1)[1] if isinstance(t, int) else ATT_TK, (b - 1) % ATT_NBUF)

    first_steady, n_steady = 2, max(n_lat_chunks - 2, 0)
    trips = n_steady // ATT_NBUF
    for t in range(first_steady):
        step(t, t % ATT_NBUF)

    def trip(i, carry):
        for u in range(ATT_NBUF):
            step(first_steady + ATT_NBUF * i + u, (first_steady + u) % ATT_NBUF)
        return carry

    lax.fori_loop(0, trips, trip, 0)
    for t in range(first_steady + trips * ATT_NBUF, n_chunks + 2):
        step(t, t % ATT_NBUF)

    dl = dl_ref[...]
    lam = (jnp.exp(jnp.sum(dl[0:1] * dl[1:2], axis=-1, keepdims=True))
           - jnp.exp(jnp.sum(dl[2:3] * dl[3:4], axis=-1, keepdims=True)) + lam_init)
    l0 = jnp.sum(l_ref[0], axis=-1, keepdims=True)
    l1 = jnp.sum(l_ref[1], axis=-1, keepdims=True)
    o = acc_ref[0] / l0 - lam * (acc_ref[1] / l1)
    o = o * lax.rsqrt(jnp.mean(o * o, axis=-1, keepdims=True) + RMS_EPS) * sub_ref[...]
    o_ref[...] = (o * (1.0 - lam_init)).astype(BF16)


def _attn_call(q, k, v, dl, subln, lam_init, *, tq, q_tile0, n_q_tiles, n_lat_chunks, ctx_start, n_ctx_keys):
    n = k.shape[0]
    kern = functools.partial(_attn_kernel, n_lat_chunks=n_lat_chunks, ctx_start=ctx_start,
                             n_ctx_keys=n_ctx_keys, lam_init=lam_init)
    resident = lambda: pl.BlockSpec((n, DIFF_V_DIM), lambda h, i: (0, h), pipeline_mode=pl.Buffered(1))
    return pl.pallas_call(
        kern,
        grid=(DIFF_HEADS, n_q_tiles),
        in_specs=[
            pl.BlockSpec((tq, DIFF_V_DIM), lambda h, i: (q_tile0 + i, h)),
            resident(),
            resident(),
            pl.BlockSpec((4, DIFF_QK_DIM), lambda h, i: (0, 0)),
            pl.BlockSpec((1, DIFF_V_DIM), lambda h, i: (0, 0)),
        ],
        out_specs=pl.BlockSpec((tq, DIFF_V_DIM), lambda h, i: (i, h)),
        out_shape=jax.ShapeDtypeStruct((n_q_tiles * tq, DIFF_WIDTH), BF16),
        scratch_shapes=[pltpu.VMEM((2, tq, DIFF_V_DIM), F32), pltpu.VMEM((2, tq, LANES), F32),
                        pltpu.VMEM((2, tq, LANES), F32), pltpu.VMEM((ATT_NBUF, 2, tq, LANES), F32),
                        pltpu.VMEM((ATT_NBUF, 2, tq, ATT_TK), F32),
                        pltpu.VMEM((ATT_NBUF, 2, tq, ATT_TK), BF16)],
        compiler_params=_cparams(("arbitrary", "arbitrary")),
        name="diff_attention",
    )(q, k, v, dl, subln)


def _diff_attention(q, k, v, dl, subln, n_lat, lam_init):
    n = q.shape[0]
    n_ctx = n - n_lat
    o_lat = _attn_call(q, k, v, dl, subln, lam_init, tq=ATT_TQ, q_tile0=0, n_q_tiles=n_lat // ATT_TQ,
                       n_lat_chunks=n_lat // ATT_TK, ctx_start=n_lat, n_ctx_keys=n_ctx)
    o_ctx = _attn_call(q, k, v, dl, subln, lam_init, tq=n_ctx, q_tile0=n_lat // n_ctx, n_q_tiles=1,
                       n_lat_chunks=0, ctx_start=n_lat, n_ctx_keys=n_ctx)
    return jnp.concatenate([o_lat, o_ctx], axis=0)


def _odd_out_kernel(on_ref, xin_ref, gb_ref, gc_ref, xinp_ref, gcp_ref, xinn_ref, gcn_ref,
                    cw_ref, x_ref, m_ref, wout_ref, g_ref, b_ref, o_ref, e_ref, cat_ref,
                    *, nlat, ntiles):
    i = pl.program_id(0)
    first, last, _, _ = _seq_flags(i, nlat, ntiles)
    u = gc_ref[...] * xin_ref[...]
    e_ref[0:HALO, :] = jnp.where(first, 0.0, gcp_ref[...] * xinp_ref[...])
    e_ref[HALO:HALO + T, :] = u
    e_ref[HALO + T:HALO + T + HALO, :] = jnp.where(last, 0.0, gcn_ref[...] * xinn_ref[...])
    cw = cw_ref[...]
    z = (cw[0:1, :] * e_ref[HALO - 1:HALO - 1 + T, :] + cw[1:2, :] * u
         + cw[2:3, :] * e_ref[HALO + 1:HALO + 1 + T, :])
    cat_ref[:, 0:DIFF_WIDTH] = on_ref[...]
    cat_ref[:, DIFF_WIDTH:D] = (gb_ref[...] * z).astype(BF16)
    y = _dot(cat_ref[...], wout_ref[...])
    o_ref[...] = _residual_ln(x_ref[...], y, m_ref[0], 2, g_ref[...], b_ref[...])


def _odd_out(on, p, conv_w, x, mod, wout_bf, g, b, nlat):
    n = x.shape[0]
    ntiles = n // T
    hb = T // HALO
    nhb = n // HALO
    kern = functools.partial(_odd_out_kernel, nlat=nlat, ntiles=ntiles)
    const2 = lambda i: (0, 0)
    blk = lambda c: pl.BlockSpec((T, CONV_WIDTH), lambda i: (i, c))
    prev = lambda c: pl.BlockSpec((HALO, CONV_WIDTH), lambda i: (jnp.maximum(i * hb - 1, 0), c))
    nxt = lambda c: pl.BlockSpec((HALO, CONV_WIDTH), lambda i: (jnp.minimum((i + 1) * hb, nhb - 1), c))
    return pl.pallas_call(
        kern,
        grid=(ntiles,),
        in_specs=[
            pl.BlockSpec((T, DIFF_WIDTH), lambda i: (i, 0)),
            blk(0), blk(1), blk(2), prev(0), prev(2), nxt(0), nxt(2),
            pl.BlockSpec((3, CONV_WIDTH), const2),
            pl.BlockSpec((T, D), lambda i: (i, 0)),
            pl.BlockSpec((1, MOD_ROWS, D), lambda i: (i // nlat, 0, 0)),
            pl.BlockSpec((D, D), const2),
            pl.BlockSpec((1, D), const2),
            pl.BlockSpec((1, D), const2),
        ],
        out_specs=pl.BlockSpec((T, D), lambda i: (i, 0)),
        out_shape=jax.ShapeDtypeStruct((n, D), F32),
        scratch_shapes=[pltpu.VMEM((T + 2 * HALO, CONV_WIDTH), F32), pltpu.VMEM((T, D), BF16)],
        compiler_params=_cparams(("arbitrary",)),
        name="odd_out",
    )(on, p, p, p, p, p, p, p, conv_w, x, mod, wout_bf, g, b)


def _router_kernel(x_ref, m_ref, rw_ref, rb_ref, eidx_ref, gate_ref, rank_ref, cnt_ref, base_ref):
    i = pl.program_id(0)

    @pl.when(i == 0)
    def _():
        base_ref[...] = jnp.zeros(base_ref.shape, F32)

    m = m_ref[0]
    h = x_ref[...] * (1.0 + m[4:5, :]) + m[3:4, :]
    h_hi = h.astype(BF16)
    h_lo = (h - h_hi.astype(F32)).astype(BF16)
    rw = rw_ref[...]
    rw_hi = rw.astype(BF16)
    rw_lo = (rw - rw_hi.astype(F32)).astype(BF16)
    logits = _dot_nt(rw_hi, h_hi) + (_dot_nt(rw_hi, h_lo) + _dot_nt(rw_lo, h_hi))
    scores = _sigmoid(logits)
    biased = scores + rb_ref[...]

    neg = -jnp.inf
    b3 = biased.reshape(N_GROUPS, GROUP_SIZE, T)
    io3 = lax.broadcasted_iota(I32, b3.shape, 1)
    m1 = jnp.max(b3, axis=1, keepdims=True)
    f1 = jnp.min(jnp.where(b3 == m1, io3, GROUP_SIZE), axis=1, keepdims=True)
    m2 = jnp.max(jnp.where(io3 == f1, neg, b3), axis=1, keepdims=True)
    gs = (m1 + m2).reshape(N_GROUPS, T)

    gio = lax.broadcasted_iota(I32, gs.shape, 0)
    gsel = jnp.zeros(gs.shape, F32)
    for _ in range(TOPK_GROUPS):
        mx = jnp.max(gs, axis=0, keepdims=True)
        f = jnp.min(jnp.where(gs == mx, gio, N_GROUPS), axis=0, keepdims=True)
        hit = gio == f
        gsel = jnp.where(hit, 1.0, gsel)
        gs = jnp.where(hit, neg, gs)
    masked = jnp.where(gsel.reshape(N_GROUPS, 1, T) > 0.5, b3, neg).reshape(N_EXPERTS, T)

    eio = lax.broadcasted_iota(I32, masked.shape, 0)
    hits, gates, eids = [], [], []
    onehot = jnp.zeros(masked.shape, F32)
    for _ in range(TOP_K):
        mx = jnp.max(masked, axis=0, keepdims=True)
        f = jnp.min(jnp.where(masked == mx, eio, N_EXPERTS), axis=0, keepdims=True)
        hit = eio == f
        hits.append(hit)
        eids.append(f)
        gates.append(jnp.sum(jnp.where(hit, scores, 0.0), axis=0, keepdims=True))
        onehot = jnp.where(hit, 1.0, onehot)
        masked = jnp.where(hit, neg, masked)
    gsum = gates[0]
    for gk in gates[1:]:
        gsum = gsum + gk

    r_io = lax.broadcasted_iota(I32, (T, T), 0)
    c_io = lax.broadcasted_iota(I32, (T, T), 1)
    upper = jnp.where(r_io < c_io, 1.0, 0.0).astype(BF16)
    before = base_ref[...] + _dot(onehot.astype(BF16), upper)
    for k in range(TOP_K):
        eidx_ref[k:k + 1, :] = eids[k]
        gate_ref[k:k + 1, :] = gates[k] / gsum * ROUTED_SCALE
        rank_ref[k:k + 1, :] = jnp.sum(jnp.where(hits[k], before, 0.0), axis=0, keepdims=True).astype(I32)
    base_new = base_ref[...] + jnp.sum(onehot, axis=1, keepdims=True)
    base_ref[...] = base_new
    cnt_ref[...] = jnp.broadcast_to(base_new, cnt_ref.shape)


def _router(x, mod, rw_t, rb_col, nlat):
    n = x.shape[0]
    kt = pl.BlockSpec((TOP_K, T), lambda i: (0, i))
    return pl.pallas_call(
        _router_kernel,
        grid=(n // T,),
        in_specs=[
            pl.BlockSpec((T, D), lambda i: (i, 0)),
            pl.BlockSpec((1, MOD_ROWS, D), lambda i: (i // nlat, 0, 0)),
            pl.BlockSpec((N_EXPERTS, D), lambda i: (0, 0)),
            pl.BlockSpec((N_EXPERTS, 1), lambda i: (0, 0)),
        ],
        out_specs=[kt, kt, kt, pl.BlockSpec((N_EXPERTS, 128), lambda i: (0, 0))],
        out_shape=[jax.ShapeDtypeStruct((TOP_K, n), I32), jax.ShapeDtypeStruct((TOP_K, n), F32),
                   jax.ShapeDtypeStruct((TOP_K, n), I32), jax.ShapeDtypeStruct((N_EXPERTS, 128), F32)],
        scratch_shapes=[pltpu.VMEM((N_EXPERTS, 1), F32)],
        compiler_params=_cparams(("arbitrary",)),
        name="moe_router",
    )(x, mod, rw_t, rb_col)


def _row_copy(src, src_row, dst, dst_row, sem):
    return pltpu.make_async_copy(src.at[pl.ds(src_row, 1), :], dst.at[pl.ds(dst_row, 1), :], sem)


def _pack_bf16_pair(v):
    half = v.shape[1] // 2
    lo = pltpu.bitcast(v[:, :half].astype(BF16).astype(F32), U32)
    hi = pltpu.bitcast(v[:, half:].astype(BF16).astype(F32), U32)
    return hi | (lo >> 16)


def _unpack_bf16_pair(w):
    lo = pltpu.bitcast(w << 16, F32)
    hi = pltpu.bitcast(w & jnp.uint32(0xFFFF0000), F32)
    return lo, hi


def _dispatch_kernel(dest_ref, x_ref, m_ref, xs_ref, h_ref, sem):
    i = pl.program_id(0)
    slot = i % 2
    m = m_ref[0]
    h_ref[slot] = _pack_bf16_pair(x_ref[...] * (1.0 + m[4:5, :]) + m[3:4, :])

    def body(tb, carry):
        for u in range(DMA_UNROLL):
            t = tb * DMA_UNROLL + u
            for k in range(TOP_K):
                _row_copy(h_ref.at[slot], t, xs_ref, dest_ref[0, k, t], sem.at[slot]).start(priority=k % 2)
        return carry

    lax.fori_loop(0, T // DMA_UNROLL, body, 0)

    def drain(s):
        for _ in range(TOP_K):
            pltpu.make_async_copy(h_ref.at[s], xs_ref.at[pl.ds(0, T), :], sem.at[s]).wait()

    @pl.when(i > 0)
    def _():
        drain(1 - slot)

    @pl.when(i == pl.num_programs(0) - 1)
    def _():
        drain(slot)


def _dispatch(dest3, x, mod, nlat):
    n = x.shape[0]
    return pl.pallas_call(
        _dispatch_kernel,
        grid=(n // T,),
        in_specs=[
            pl.BlockSpec((1, TOP_K, T), lambda i: (i, 0, 0), memory_space=pltpu.SMEM),
            pl.BlockSpec((T, D), lambda i: (i, 0)),
            pl.BlockSpec((1, MOD_ROWS, D), lambda i: (i // nlat, 0, 0)),
        ],
        out_specs=pl.BlockSpec(memory_space=pl.ANY),
        out_shape=jax.ShapeDtypeStruct((n * TOP_K, D // 2), U32),
        scratch_shapes=[pltpu.VMEM((2, T, D // 2), U32), pltpu.SemaphoreType.DMA((2,))],
        compiler_params=_cparams(("arbitrary",)),
        name="moe_dispatch",
    )(dest3, x, mod)


def _expert_kernel(vt_ref, ve_ref, vlo_ref, vhi_ref, vnew_ref, nv_ref, xs_ref, wg_ref, wu_ref, wd_ref,
                   ys_ref, wgu_bf, wd_bf):
    v = pl.program_id(0)

    @pl.when(v < nv_ref[0])
    def _():
        @pl.when(vnew_ref[v] == 1)
        def _():
            wgu_bf[:, 0:EXPERT_FF] = wg_ref[0, 0].astype(BF16)
            wgu_bf[:, EXPERT_FF:2 * EXPERT_FF] = wu_ref[0, 0].astype(BF16)
            wd_bf[...] = wd_ref[0, 0].astype(BF16)

        def ffn(r):
            x_lo, x_hi = _unpack_bf16_pair(xs_ref[r * EXPERT_RB:(r + 1) * EXPERT_RB, :])
            xb = jnp.concatenate([x_lo.astype(BF16), x_hi.astype(BF16)], axis=1)
            h12 = _dot(xb, wgu_bf[...])
            a = (_silu(h12[:, 0:EXPERT_FF]) * h12[:, EXPERT_FF:2 * EXPERT_FF]).astype(BF16)
            return _pack_bf16_pair(_dot(a, wd_bf[...]))

        lo = vlo_ref[v]
        hi = vhi_ref[v]
        whole = jnp.logical_and(lo == 0, hi == TS)

        @pl.when(whole)
        def _():
            for r in range(TS // EXPERT_RB):
                ys_ref[r * EXPERT_RB:(r + 1) * EXPERT_RB, :] = ffn(r)

        @pl.when(jnp.logical_not(whole))
        def _():
            for r in range(TS // EXPERT_RB):
                c0, c1 = r * EXPERT_RB, (r + 1) * EXPERT_RB
                blk = slice(c0, c1)

                @pl.when(jnp.logical_and(lo < c1, hi > c0))
                def _():
                    y = ffn(r)
                    row = c0 + lax.broadcasted_iota(I32, (EXPERT_RB, 1), 0)
                    mine = jnp.logical_and(row >= lo, row < hi)
                    starts_before = lo <= c0

                    @pl.when(jnp.logical_and(starts_before, hi >= c1))
                    def _():
                        ys_ref[blk, :] = y

                    @pl.when(jnp.logical_and(starts_before, hi < c1))
                    def _():
                        ys_ref[blk, :] = jnp.where(mine, y, jnp.uint32(0))

                    @pl.when(jnp.logical_not(starts_before))
                    def _():
                        ys_ref[blk, :] = jnp.where(mine, y, ys_ref[blk, :])


def _expert_ffn(sched, xs, w_gate, w_up, w_down, li):
    vt, ve, vlo, vhi, vnew, nv = sched
    nslots = xs.shape[0]
    nvis = vt.shape[0]
    w_in = lambda v, vt, ve, *_: (li, ve[v], 0, 0)
    rows = lambda v, vt, *_: (vt[v], 0)
    grid_spec = pltpu.PrefetchScalarGridSpec(
        num_scalar_prefetch=6,
        grid=(nvis,),
        in_specs=[
            pl.BlockSpec((TS, D // 2), rows),
            pl.BlockSpec((1, 1, D, EXPERT_FF), w_in),
            pl.BlockSpec((1, 1, D, EXPERT_FF), w_in),
            pl.BlockSpec((1, 1, EXPERT_FF, D), w_in),
        ],
        out_specs=pl.BlockSpec((TS, D // 2), rows),
        scratch_shapes=[pltpu.VMEM((D, 2 * EXPERT_FF), BF16), pltpu.VMEM((EXPERT_FF, D), BF16)],
    )
    return pl.pallas_call(
        _expert_kernel,
        grid_spec=grid_spec,
        out_shape=jax.ShapeDtypeStruct((nslots, D // 2), U32),
        compiler_params=_cparams(("arbitrary",)),
        name="moe_experts",
    )(vt, ve, vlo, vhi, vnew, nv, xs, w_gate, w_up, w_down)


def _combine_kernel(dest_ref, ys_ref, x_ref, m_ref, gt_ref, sg_ref, su_ref, sd_ref, g_ref, b_ref,
                    o_ref, *scratch):
    bufs, sem = scratch[:TOP_K], scratch[TOP_K]

    def body(tb, carry):
        for u in range(DMA_UNROLL):
            t = tb * DMA_UNROLL + u
            for k in range(TOP_K):
                _row_copy(ys_ref, dest_ref[0, k, t], bufs[k], t, sem).start(priority=k % 2)
        return carry

    lax.fori_loop(0, T // DMA_UNROLL, body, 0)

    m = m_ref[0]
    x = x_ref[...]
    hb = (x * (1.0 + m[4:5, :]) + m[3:4, :]).astype(BF16)
    f = _dot((_silu(_dot(hb, sg_ref[...])) * _dot(hb, su_ref[...])).astype(BF16), sd_ref[...])

    for k in range(TOP_K):
        pltpu.make_async_copy(ys_ref.at[pl.ds(0, T), :], bufs[k], sem).wait()
    gt = gt_ref[...]
    f_lo = f[:, :D // 2]
    f_hi = f[:, D // 2:]
    for k in range(TOP_K):
        y_lo, y_hi = _unpack_bf16_pair(bufs[k][...])
        f_lo = f_lo + gt[:, k:k + 1] * y_lo
        f_hi = f_hi + gt[:, k:k + 1] * y_hi
    f = jnp.concatenate([f_lo, f_hi], axis=1)
    o_ref[...] = _residual_ln(x, f, m, 5, g_ref[...], b_ref[...])


def _combine(dest3, ys, x, mod, gates_t, sg_bf, su_bf, sd_bf, g, b, nlat, n):
    const2 = lambda i: (0, 0)
    return pl.pallas_call(
        _combine_kernel,
        grid=(n // T,),
        in_specs=[
            pl.BlockSpec((1, TOP_K, T), lambda i: (i, 0, 0), memory_space=pltpu.SMEM),
            pl.BlockSpec(memory_space=pl.ANY),
            pl.BlockSpec((T, D), lambda i: (i, 0)),
            pl.BlockSpec((1, MOD_ROWS, D), lambda i: (i // nlat, 0, 0)),
            pl.BlockSpec((T, TOP_K), lambda i: (i, 0)),
            pl.BlockSpec((D, EXPERT_FF), const2),
            pl.BlockSpec((D, EXPERT_FF), const2),
            pl.BlockSpec((EXPERT_FF, D), const2),
            pl.BlockSpec((1, D), const2),
            pl.BlockSpec((1, D), const2),
        ],
        out_specs=pl.BlockSpec((T, D), lambda i: (i, 0)),
        out_shape=jax.ShapeDtypeStruct((n, D), F32),
        scratch_shapes=[pltpu.VMEM((T, D // 2), U32)] * TOP_K + [pltpu.SemaphoreType.DMA],
        compiler_params=_cparams(("arbitrary",)),
        name="moe_combine",
    )(dest3, ys, x, mod, gates_t, sg_bf, su_bf, sd_bf, g, b)


def _slot_schedule(counts, n_slot_tiles):
    offs = jnp.cumsum(counts) - counts
    ends = offs + counts
    first_tile = offs // TS
    last_tile = jnp.maximum(ends - 1, offs) // TS
    nvis_e = jnp.where(counts > 0, last_tile - first_tile + 1, 0)
    vstart = jnp.cumsum(nvis_e) - nvis_e
    total = jnp.sum(nvis_e)
    nvis = n_slot_tiles + N_EXPERTS - 1
    v = jnp.arange(nvis, dtype=I32)
    experts = jnp.arange(N_EXPERTS, dtype=I32)
    e = jnp.minimum(jnp.sum((vstart + nvis_e)[None, :] <= v[:, None], axis=1), N_EXPERTS - 1).astype(I32)
    onehot = e[:, None] == experts[None, :]
    pick = lambda a: jnp.sum(jnp.where(onehot, a[None, :], 0), axis=1)
    tile = pick(first_tile) + (v - pick(vstart))
    lo = jnp.maximum(pick(offs), tile * TS) - tile * TS
    hi = jnp.minimum(pick(ends), (tile + 1) * TS) - tile * TS
    valid = v < total
    e_last = jnp.max(jnp.where(counts > 0, experts, 0))
    vt = jnp.where(valid, tile, n_slot_tiles - 1).astype(I32)
    ve = jnp.where(valid, e, e_last).astype(I32)
    vlo = jnp.where(valid, lo, 0).astype(I32)
    vhi = jnp.where(valid, hi, 0).astype(I32)
    vnew = jnp.concatenate([jnp.ones((1,), I32), (ve[1:] != ve[:-1]).astype(I32)])
    return offs, (vt, ve, vlo, vhi, vnew, total.astype(I32).reshape(1))


def _moe_layer(x, mod, rw_t, rb_col, w_gate, w_up, w_down, li, sg_bf, su_bf, sd_bf, g, b, nlat, n_out):
    n = x.shape[0]
    eidx, gates, rank, cnt = _router(x, mod, rw_t, rb_col, nlat)
    counts = cnt[:, 0].astype(I32)
    offs, sched = _slot_schedule(counts, n * TOP_K // TS)
    experts = jnp.arange(N_EXPERTS, dtype=I32)
    dest = rank + jnp.sum(jnp.where(eidx[..., None] == experts, offs, 0), axis=-1)
    dest3 = dest.reshape(TOP_K, n // T, T).transpose(1, 0, 2)
    xs = _dispatch(dest3, x, mod, nlat)
    ys = _expert_ffn(sched, xs, w_gate, w_up, w_down, li)
    return _combine(dest3, ys, x, mod, gates.T, sg_bf, su_bf, sd_bf, g, b, nlat, n_out)


def _rope_tables(n_lat, n_ctx):
    rows = n_lat // GRID_W
    r, cidx = jnp.meshgrid(jnp.arange(rows), jnp.arange(GRID_W), indexing="ij")
    pos = jnp.stack([r.reshape(-1), cidx.reshape(-1)], axis=-1).astype(F32)
    nf = DIFF_QK_DIM // 4
    inv = ROPE_BASE ** (-jnp.arange(nf, dtype=F32) / nf)
    ang = jnp.broadcast_to(pos[:, :, None, None] * inv, (n_lat, 2, 2, nf)).reshape(n_lat, DIFF_QK_DIM)
    cos = jnp.concatenate([jnp.cos(ang), jnp.ones((n_ctx, DIFF_QK_DIM), F32)], axis=0)
    sin = jnp.concatenate([jnp.sin(ang), jnp.zeros((n_ctx, DIFF_QK_DIM), F32)], axis=0)
    return cos, sin


def kernel(x, c, ctx, c_ctx, ada_w, ada_b, ln_g, ln_b, ev_w_in, ev_w_out, pool_w, pool_scale, sgu_ln_g, sgu_ln_b, sgu_w, sgu_b, od_w_in, od_w_out, diff_lambda, diff_subln, conv_w, router_w, router_bias, exp_w_gate, exp_w_up, exp_w_down, sh_w_gate, sh_w_up, sh_w_down):
    bsz, n_lat, d = x.shape
    n_ctx = ctx.shape[1]
    assert bsz == 1 and d == D and n_lat % T == 0 and n_ctx % T == 0
    assert n_lat % ATT_TK == 0 and n_lat % ATT_TQ == 0 and n_lat % n_ctx == 0 and n_ctx % LANES == 0
    depth = ada_w.shape[0]
    nlat = n_lat // T

    xs = jnp.concatenate([x[0], ctx[0]], axis=0)
    mods = _ada_mods(jnp.stack([c[0], c_ctx], axis=1), ada_w, ada_b)
    cos, sin = _rope_tables(n_lat, n_ctx)
    eye = jnp.eye(T // SGU_CHUNK, dtype=F32)

    for li in range(depth):
        j = li // 2
        mod = mods[li]
        g0, b0 = ln_g[li, 0][None], ln_b[li, 0][None]
        g1, b1 = ln_g[li, 1][None], ln_b[li, 1][None]
        if li % 2 == 0:
            p = _mod_matmul(xs, mod, ev_w_in[j].astype(BF16), nlat, sh=0, sc=1, tn=EVEN_IN)
            sguw_bd = jnp.einsum("ab,hpq->hapbq", eye, sgu_w[j]).reshape(SGU_HEADS, T, T).astype(BF16)
            sgub_t = jnp.tile(sgu_b[j].T, (T // SGU_CHUNK, 1))
            xs = _even_mix(p, xs, mod, pool_w[j].astype(BF16), pool_scale[j], sgu_ln_g[j][None], sgu_ln_b[j][None],
                           sguw_bd, sgub_t, ev_w_out[j].astype(BF16), g0, b0, nlat)
        else:
            q, k, v = _qkv_proj(xs, mod, od_w_in[j][:, :3 * DIFF_WIDTH].astype(BF16), cos, sin, nlat)
            p = _mod_matmul(xs, mod, od_w_in[j][:, 3 * DIFF_WIDTH:].astype(BF16), nlat, sh=0, sc=1,
                            tn=3 * CONV_WIDTH)
            lam_init = 0.8 - 0.6 * math.exp(-0.3 * li)
            on = _diff_attention(q, k, v, diff_lambda[j], diff_subln[j][None], n_lat, lam_init)
            xs = _odd_out(on, p, conv_w[j], xs, mod, od_w_out[j].astype(BF16), g0, b0, nlat)
        xs = _moe_layer(xs, mod, router_w[li].T, router_bias[li][:, None], exp_w_gate, exp_w_up,
                        exp_w_down, li, sh_w_gate[li].astype(BF16), sh_w_up[li].astype(BF16),
                        sh_w_down[li].astype(BF16), g1, b1, nlat,
                        n_out=n_lat if li == depth - 1 else n_lat + n_ctx)
    return xs[None]
```

```python
import functools
import math

import jax
import jax.numpy as jnp
from jax import lax
from jax.experimental import pallas as pl
from jax.experimental.pallas import tpu as pltpu

F32 = jnp.float32
BF16 = jnp.bfloat16
I32 = jnp.int32
U32 = jnp.uint32

D = 2048
DEPTH = 4
GRID_W = 64

POOL_GROUPS = 4
POOL_WINDOWS = (2, 4, 8, 16)
POOL_WIDTH = 1024
POOL_GROUP_DIM = 256
SGU_WIDTH = 1024
SGU_CHUNK = 128
SGU_HEADS = 8
EVEN_IN = 3072

DIFF_HEADS = 4
DIFF_QK_DIM = 128
DIFF_V_DIM = 256
DIFF_WIDTH = 1024
CONV_WIDTH = 1024
ODD_IN = 6144
ROPE_BASE = 10000.0

N_EXPERTS = 64
EXPERT_FF = 384
TOP_K = 8
N_GROUPS = 8
GROUP_SIZE = 8
TOPK_GROUPS = 4
ROUTED_SCALE = 2.5

ALPHA = (2 * DEPTH) ** 0.25
LN_EPS = 1e-6
RMS_EPS = 1e-5

T = 256
TS = 512
EXPERT_RB = 128
DMA_UNROLL = 16
HALO = 8
MOD_ROWS = 8
ADA_TN = 1536
ADA_UNROLL = 8
ATT_TQ = 512
ATT_TK = 512
ATT_RB = 64
ATT_NBUF = 3
ATT_TRIP = 3
LANES = 128
LOG2E = 1.4426950408889634
VMEM_LIMIT = 56 * 1024 * 1024


def _cparams(sem, vmem=VMEM_LIMIT, flags=None):
    return pltpu.CompilerParams(dimension_semantics=sem, vmem_limit_bytes=vmem, flags=flags)


def _sigmoid(x):
    return 1.0 / (1.0 + jnp.exp(-x))


def _silu(x):
    return x * _sigmoid(x)


def _layer_norm(v, g, b):
    mu = jnp.mean(v, axis=-1, keepdims=True)
    d = v - mu
    var = jnp.mean(d * d, axis=-1, keepdims=True)
    return d * lax.rsqrt(var + LN_EPS) * g + b


def _dot(a, b):
    return jnp.dot(a, b, preferred_element_type=F32)


def _dot_nt(a, b):
    return lax.dot_general(a, b, (((1,), (1,)), ((), ())), preferred_element_type=F32)


def _ada_kernel(c_ref, w_ref, b_ref, o_ref, s_ref):
    tn = w_ref.shape[2]
    s_ref[...] = _silu(c_ref[...])

    def body(m, carry):
        a0, a1 = carry
        for u in range(ADA_UNROLL):
            r = pl.multiple_of(m * (8 * ADA_UNROLL) + 8 * u, 8)
            w = w_ref[0, pl.ds(r, 8), :]
            s = s_ref[pl.ds(r, 8), :]
            a0 = a0 + w * s[:, 0:1]
            a1 = a1 + w * s[:, 1:2]
        return a0, a1

    z = jnp.zeros((8, tn), F32)
    a0, a1 = lax.fori_loop(0, D // (8 * ADA_UNROLL), body, (z, z))
    bias = b_ref[0]
    o_ref[0] = jnp.zeros((8, tn), F32)
    o_ref[0, 0:1, :] = jnp.sum(a0, axis=0, keepdims=True) + bias
    o_ref[0, 1:2, :] = jnp.sum(a1, axis=0, keepdims=True) + bias


def _ada_mods(c_cols, ada_w, ada_b):
    depth = ada_w.shape[0]
    n6 = ada_w.shape[2]
    out = pl.pallas_call(
        _ada_kernel,
        grid=(depth, n6 // ADA_TN),
        in_specs=[
            pl.BlockSpec((D, 2), lambda l, j: (0, 0)),
            pl.BlockSpec((1, D, ADA_TN), lambda l, j: (l, 0, j)),
            pl.BlockSpec((1, 1, ADA_TN), lambda l, j: (l, 0, j)),
        ],
        out_specs=pl.BlockSpec((1, 8, ADA_TN), lambda l, j: (l, 0, j)),
        out_shape=jax.ShapeDtypeStruct((depth, 8, n6), F32),
        scratch_shapes=[pltpu.VMEM((D, 2), F32)],
        compiler_params=_cparams(("arbitrary", "arbitrary")),
        name="ada_mods",
    )(c_cols, ada_w, ada_b.reshape(depth, 1, n6))
    mods = out[:, :2, :].reshape(depth, 2, 6, D)
    return jnp.pad(mods, ((0, 0), (0, 0), (0, MOD_ROWS - 6), (0, 0)))


def _modmm_kernel(x_ref, m_ref, w_ref, o_ref, *, sh, sc):
    m = m_ref[0]
    h = x_ref[...] * (1.0 + m[sc:sc + 1, :]) + m[sh:sh + 1, :]
    o_ref[...] = _dot(h.astype(BF16), w_ref[...])


def _mod_matmul(x, mod, w_bf, nlat, *, sh, sc, tn):
    n = x.shape[0]
    nout = w_bf.shape[1]
    return pl.pallas_call(
        functools.partial(_modmm_kernel, sh=sh, sc=sc),
        grid=(nout // tn, n // T),
        in_specs=[
            pl.BlockSpec((T, D), lambda j, i: (i, 0)),
            pl.BlockSpec((1, MOD_ROWS, D), lambda j, i: (i // nlat, 0, 0)),
            pl.BlockSpec((D, tn), lambda j, i: (0, j)),
        ],
        out_specs=pl.BlockSpec((T, tn), lambda j, i: (i, j)),
        out_shape=jax.ShapeDtypeStruct((n, nout), F32),
        compiler_params=_cparams(("arbitrary", "arbitrary")),
        name="mod_matmul",
    )(x, mod, w_bf)


def _seq_flags(i, nlat, ntiles):
    is_ctx = i >= nlat
    first = jnp.logical_or(i == 0, i == nlat)
    last = jnp.logical_or(i == nlat - 1, i == ntiles - 1)
    t0 = jnp.where(is_ctx, i - nlat, i) * T
    nseq = jnp.where(is_ctx, (ntiles - nlat) * T, nlat * T)
    return first, last, t0, nseq


def _residual_ln(x, y, m, gate_row, g, b):
    return _layer_norm(ALPHA * x + m[gate_row:gate_row + 1, :] * y, g, b)


def _even_mix_kernel(pp_ref, pprev_ref, pnext_ref, pu_ref, pv_ref, x_ref, m_ref,
                     poolw_ref, pscale_ref, slng_ref, slnb_ref, sguw_ref, sgub_ref,
                     wout_ref, g_ref, b_ref, o_ref, e_ref, cat_ref, *, nlat, ntiles):
    i = pl.program_id(0)
    first, last, t0, nseq = _seq_flags(i, nlat, ntiles)

    e_ref[0:HALO, :] = jnp.where(first, 0.0, pprev_ref[...])
    e_ref[HALO:HALO + T, :] = pp_ref[...]
    e_ref[HALO + T:HALO + T + HALO, :] = jnp.where(last, 0.0, pnext_ref[...])

    pos = t0 + lax.broadcasted_iota(I32, (T, 1), 0)
    for g in range(POOL_GROUPS):
        w = POOL_WINDOWS[g]
        lo, hi = w // 2, w - 1 - w // 2
        cols = slice(g * POOL_GROUP_DIM, (g + 1) * POOL_GROUP_DIM)
        tot = e_ref[HALO - lo:HALO - lo + T, cols]
        for d in range(-lo + 1, hi + 1):
            tot = tot + e_ref[HALO + d:HALO + d + T, cols]
        cnt = jnp.minimum(pos + hi, nseq - 1) - jnp.maximum(pos - lo, 0) + 1
        pooled = tot / cnt.astype(F32) - e_ref[HALO:HALO + T, cols]
        a = _dot(pooled.astype(BF16), poolw_ref[g]) * pscale_ref[g:g + 1, :]
        cat_ref[:, cols] = a.astype(BF16)

    zu = jax.nn.gelu(pu_ref[...])
    v = _layer_norm(jax.nn.gelu(pv_ref[...]), slng_ref[...], slnb_ref[...]).astype(BF16)
    for h in range(SGU_HEADS):
        cols = slice(h * SGU_CHUNK, (h + 1) * SGU_CHUNK)
        mixed = _dot(sguw_ref[h], v[:, cols]) + sgub_ref[:, h:h + 1]
        cat_ref[:, POOL_WIDTH + h * SGU_CHUNK:POOL_WIDTH + (h + 1) * SGU_CHUNK] = (zu[:, cols] * mixed).astype(BF16)

    y = _dot(cat_ref[...], wout_ref[...])
    o_ref[...] = _residual_ln(x_ref[...], y, m_ref[0], 2, g_ref[...], b_ref[...])


def _even_mix(p, x, mod, poolw_bf, pscale, slng, slnb, sguw_bd, sgub_t, wout_bf, g, b, nlat):
    n = x.shape[0]
    ntiles = n // T
    hb = T // HALO
    nhb = n // HALO
    kern = functools.partial(_even_mix_kernel, nlat=nlat, ntiles=ntiles)
    const2 = lambda i: (0, 0)
    const3 = lambda i: (0, 0, 0)
    return pl.pallas_call(
        kern,
        grid=(ntiles,),
        in_specs=[
            pl.BlockSpec((T, POOL_WIDTH), lambda i: (i, 0)),
            pl.BlockSpec((HALO, POOL_WIDTH), lambda i: (jnp.maximum(i * hb - 1, 0), 0)),
            pl.BlockSpec((HALO, POOL_WIDTH), lambda i: (jnp.minimum((i + 1) * hb, nhb - 1), 0)),
            pl.BlockSpec((T, SGU_WIDTH), lambda i: (i, 1)),
            pl.BlockSpec((T, SGU_WIDTH), lambda i: (i, 2)),
            pl.BlockSpec((T, D), lambda i: (i, 0)),
            pl.BlockSpec((1, MOD_ROWS, D), lambda i: (i // nlat, 0, 0)),
            pl.BlockSpec((POOL_GROUPS, POOL_GROUP_DIM, POOL_GROUP_DIM), const3),
            pl.BlockSpec((POOL_GROUPS, POOL_GROUP_DIM), const2),
            pl.BlockSpec((1, SGU_WIDTH), const2),
            pl.BlockSpec((1, SGU_WIDTH), const2),
            pl.BlockSpec((SGU_HEADS, T, T), const3),
            pl.BlockSpec((T, SGU_HEADS), const2),
            pl.BlockSpec((D, D), const2),
            pl.BlockSpec((1, D), const2),
            pl.BlockSpec((1, D), const2),
        ],
        out_specs=pl.BlockSpec((T, D), lambda i: (i, 0)),
        out_shape=jax.ShapeDtypeStruct((n, D), F32),
        scratch_shapes=[pltpu.VMEM((T + 2 * HALO, POOL_WIDTH), F32), pltpu.VMEM((T, D), BF16)],
        compiler_params=_cparams(("arbitrary",)),
        name="even_mix",
    )(p, p, p, p, p, x, mod, poolw_bf, pscale, slng, slnb, sguw_bd, sgub_t, wout_bf, g, b)


def _qkv_kernel(x_ref, m_ref, w_ref, cos_ref, sin_ref, q_ref, k_ref, v_ref):
    m = m_ref[0]
    h = x_ref[...] * (1.0 + m[1:2, :]) + m[0:1, :]
    p = _dot(h.astype(BF16), w_ref[...])
    cos = cos_ref[...]
    sin = sin_ref[...]
    lane = lax.broadcasted_iota(I32, (1, DIFF_QK_DIM), 1)
    low_half = (lane % 64) < 32
    q_scale = DIFF_QK_DIM ** -0.5 * LOG2E

    def rope(x):
        rot = jnp.where(low_half, -pltpu.roll(x, 96, 1), pltpu.roll(x, 32, 1))
        return x * cos + rot * sin

    for j in range(DIFF_WIDTH // DIFF_QK_DIM):
        cols = slice(j * DIFF_QK_DIM, (j + 1) * DIFF_QK_DIM)
        kcols = slice(DIFF_WIDTH + j * DIFF_QK_DIM, DIFF_WIDTH + (j + 1) * DIFF_QK_DIM)
        q_ref[:, cols] = (rope(p[:, cols]) * q_scale).astype(BF16)
        k_ref[:, cols] = rope(p[:, kcols]).astype(BF16)
    v_ref[...] = p[:, 2 * DIFF_WIDTH:3 * DIFF_WIDTH].astype(BF16)


def _qkv_proj(x, mod, w_bf, cos, sin, nlat):
    n = x.shape[0]
    out = pl.BlockSpec((T, DIFF_WIDTH), lambda i: (i, 0))
    tab = pl.BlockSpec((T, DIFF_QK_DIM), lambda i: (i, 0))
    shp = jax.ShapeDtypeStruct((n, DIFF_WIDTH), BF16)
    return pl.pallas_call(
        _qkv_kernel,
        grid=(n // T,),
        in_specs=[
            pl.BlockSpec((T, D), lambda i: (i, 0)),
            pl.BlockSpec((1, MOD_ROWS, D), lambda i: (i // nlat, 0, 0)),
            pl.BlockSpec((D, 3 * DIFF_WIDTH), lambda i: (0, 0)),
            tab, tab,
        ],
        out_specs=[out, out, out],
        out_shape=[shp, shp, shp],
        compiler_params=_cparams(("arbitrary",)),
        name="qkv_proj",
    )(x, mod, w_bf, cos, sin)


def _attn_kernel(q_ref, k_ref, v_ref, dl_ref, sub_ref, o_ref, acc_ref, m_ref, l_ref, a_ref, s_ref, p_ref,
                 *, n_lat_chunks, ctx_start, n_ctx_keys, lam_init):
    m_ref[...] = jnp.full(m_ref.shape, -jnp.inf, F32)
    l_ref[...] = jnp.zeros(l_ref.shape, F32)
    acc_ref[...] = jnp.zeros(acc_ref.shape, F32)
    tq = q_ref.shape[0]

    def scores(start, size, b):
        for c in range(2):
            cols = slice(c * DIFF_QK_DIM, (c + 1) * DIFF_QK_DIM)
            s_ref[b, c, :, 0:size] = _dot(q_ref[:, cols], k_ref[cols, pl.ds(start, size)])

    def softmax(size, b):
        for c in range(2):
            for r in range(tq // ATT_RB):
                rows = slice(r * ATT_RB, (r + 1) * ATT_RB)
                s = s_ref[b, c, rows, 0:size]
                m_old = m_ref[c, rows, :]
                m_new = jnp.maximum(m_old, jnp.max(s, axis=-1, keepdims=True))
                alpha = jnp.exp2(m_old - m_new)
                lsum = None
                for j in range(size // LANES):
                    pj = jnp.exp2(s[:, j * LANES:(j + 1) * LANES] - m_new)
                    p_ref[b, c, rows, j * LANES:(j + 1) * LANES] = pj.astype(BF16)
                    lsum = pj if lsum is None else lsum + pj
                l_ref[c, rows, :] = alpha * l_ref[c, rows, :] + lsum
                m_ref[c, rows, :] = m_new
                a_ref[b, c, rows, :] = alpha

    def values(start, size, b):
        vv = v_ref[pl.ds(start, size), :]
        for c in range(2):
            alpha = a_ref[b, c]
            pv = _dot(p_ref[b, c, :, 0:size], vv)
            acc_ref[c] = jnp.concatenate([alpha, alpha], axis=1) * acc_ref[c] + pv

    n_chunks = n_lat_chunks + 1

    def chunk(j):
        if isinstance(j, int) and j == n_lat_chunks:
            return ctx_start, n_ctx_keys
        return (j * ATT_TK if isinstance(j, int) else pl.multiple_of(j * ATT_TK, ATT_TK)), ATT_TK

    def step(t, b):
        if not isinstance(t, int) or 0 <= t - 2 < n_chunks:
            values(*chunk(t - 2), (b - 2) % ATT_NBUF)
        if not isinstance(t, int) or t < n_chunks:
            scores(*chunk(t), b)
        if not isinstance(t, int) or 0 <= t - 1 < n_chunks:
            softmax(chunk(t - 1)[1] if isinstance(t, int) else ATT_TK, (b - 1) % ATT_NBUF)

    first_steady, n_steady = 2, max(n_lat_chunks - 2, 0)
    trips = n_steady // ATT_TRIP
    for t in range(first_steady):
        step(t, t % ATT_NBUF)

    def trip(i, carry):
        for u in range(ATT_TRIP):
            step(first_steady + ATT_TRIP * i + u, (first_steady + u) % ATT_NBUF)
        return carry

    lax.fori_loop(0, trips, trip, 0)
    for t in range(first_steady + trips * ATT_TRIP, n_chunks + 2):
        step(t, t % ATT_NBUF)

    dl = dl_ref[...]
    lam = (jnp.exp(jnp.sum(dl[0:1] * dl[1:2], axis=-1, keepdims=True))
           - jnp.exp(jnp.sum(dl[2:3] * dl[3:4], axis=-1, keepdims=True)) + lam_init)
    l0 = jnp.sum(l_ref[0], axis=-1, keepdims=True)
    l1 = jnp.sum(l_ref[1], axis=-1, keepdims=True)
    o = acc_ref[0] / l0 - lam * (acc_ref[1] / l1)
    o = o * lax.rsqrt(jnp.mean(o * o, axis=-1, keepdims=True) + RMS_EPS) * sub_ref[...]
    o_ref[...] = (o * (1.0 - lam_init)).astype(BF16)


def _attn_call(q, kt, v, dl, subln, lam_init, *, tq, q_tile0, n_q_tiles, n_lat_chunks, ctx_start, n_ctx_keys):
    n = v.shape[0]
    kern = functools.partial(_attn_kernel, n_lat_chunks=n_lat_chunks, ctx_start=ctx_start,
                             n_ctx_keys=n_ctx_keys, lam_init=lam_init)
    resident = lambda: pl.BlockSpec((n, DIFF_V_DIM), lambda h, i: (0, h), pipeline_mode=pl.Buffered(1))
    resident_t = pl.BlockSpec((DIFF_V_DIM, n), lambda h, i: (h, 0), pipeline_mode=pl.Buffered(1))
    return pl.pallas_call(
        kern,
        grid=(DIFF_HEADS, n_q_tiles),
        in_specs=[
            pl.BlockSpec((tq, DIFF_V_DIM), lambda h, i: (q_tile0 + i, h)),
            resident_t,
            resident(),
            pl.BlockSpec((4, DIFF_QK_DIM), lambda h, i: (0, 0)),
            pl.BlockSpec((1, DIFF_V_DIM), lambda h, i: (0, 0)),
        ],
        out_specs=pl.BlockSpec((tq, DIFF_V_DIM), lambda h, i: (i, h)),
        out_shape=jax.ShapeDtypeStruct((n_q_tiles * tq, DIFF_WIDTH), BF16),
        scratch_shapes=[pltpu.VMEM((2, tq, DIFF_V_DIM), F32), pltpu.VMEM((2, tq, LANES), F32),
                        pltpu.VMEM((2, tq, LANES), F32), pltpu.VMEM((ATT_NBUF, 2, tq, LANES), F32),
                        pltpu.VMEM((ATT_NBUF, 2, tq, ATT_TK), F32),
                        pltpu.VMEM((ATT_NBUF, 2, tq, ATT_TK), BF16)],
        compiler_params=_cparams(("arbitrary", "arbitrary")),
        name="diff_attention",
    )(q, kt, v, dl, subln)


def _diff_attention(q, k, v, dl, subln, n_lat, lam_init):
    n = q.shape[0]
    n_ctx = n - n_lat
    kt = k.T
    o_lat = _attn_call(q, kt, v, dl, subln, lam_init, tq=ATT_TQ, q_tile0=0, n_q_tiles=n_lat // ATT_TQ,
                       n_lat_chunks=n_lat // ATT_TK, ctx_start=n_lat, n_ctx_keys=n_ctx)
    o_ctx = _attn_call(q, kt, v, dl, subln, lam_init, tq=n_ctx, q_tile0=n_lat // n_ctx, n_q_tiles=1,
                       n_lat_chunks=0, ctx_start=n_lat, n_ctx_keys=n_ctx)
    return jnp.concatenate([o_lat, o_ctx], axis=0)


def _odd_out_kernel(on_ref, xin_ref, gb_ref, gc_ref, xinp_ref, gcp_ref, xinn_ref, gcn_ref,
                    cw_ref, x_ref, m_ref, wout_ref, g_ref, b_ref, o_ref, e_ref, cat_ref,
                    *, nlat, ntiles):
    i = pl.program_id(0)
    first, last, _, _ = _seq_flags(i, nlat, ntiles)
    u = gc_ref[...] * xin_ref[...]
    e_ref[0:HALO, :] = jnp.where(first, 0.0, gcp_ref[...] * xinp_ref[...])
    e_ref[HALO:HALO + T, :] = u
    e_ref[HALO + T:HALO + T + HALO, :] = jnp.where(last, 0.0, gcn_ref[...] * xinn_ref[...])
    cw = cw_ref[...]
    z = (cw[0:1, :] * e_ref[HALO - 1:HALO - 1 + T, :] + cw[1:2, :] * u
         + cw[2:3, :] * e_ref[HALO + 1:HALO + 1 + T, :])
    cat_ref[:, 0:DIFF_WIDTH] = on_ref[...]
    cat_ref[:, DIFF_WIDTH:D] = (gb_ref[...] * z).astype(BF16)
    y = _dot(cat_ref[...], wout_ref[...])
    o_ref[...] = _residual_ln(x_ref[...], y, m_ref[0], 2, g_ref[...], b_ref[...])


def _odd_out(on, p, conv_w, x, mod, wout_bf, g, b, nlat):
    n = x.shape[0]
    ntiles = n // T
    hb = T // HALO
    nhb = n // HALO
    kern = functools.partial(_odd_out_kernel, nlat=nlat, ntiles=ntiles)
    const2 = lambda i: (0, 0)
    blk = lambda c: pl.BlockSpec((T, CONV_WIDTH), lambda i: (i, c))
    prev = lambda c: pl.BlockSpec((HALO, CONV_WIDTH), lambda i: (jnp.maximum(i * hb - 1, 0), c))
    nxt = lambda c: pl.BlockSpec((HALO, CONV_WIDTH), lambda i: (jnp.minimum((i + 1) * hb, nhb - 1), c))
    return pl.pallas_call(
        kern,
        grid=(ntiles,),
        in_specs=[
            pl.BlockSpec((T, DIFF_WIDTH), lambda i: (i, 0)),
            blk(0), blk(1), blk(2), prev(0), prev(2), nxt(0), nxt(2),
            pl.BlockSpec((3, CONV_WIDTH), const2),
            pl.BlockSpec((T, D), lambda i: (i, 0)),
            pl.BlockSpec((1, MOD_ROWS, D), lambda i: (i // nlat, 0, 0)),
            pl.BlockSpec((D, D), const2),
            pl.BlockSpec((1, D), const2),
            pl.BlockSpec((1, D), const2),
        ],
        out_specs=pl.BlockSpec((T, D), lambda i: (i, 0)),
        out_shape=jax.ShapeDtypeStruct((n, D), F32),
        scratch_shapes=[pltpu.VMEM((T + 2 * HALO, CONV_WIDTH), F32), pltpu.VMEM((T, D), BF16)],
        compiler_params=_cparams(("arbitrary",)),
        name="odd_out",
    )(on, p, p, p, p, p, p, p, conv_w, x, mod, wout_bf, g, b)


def _router_kernel(x_ref, m_ref, rw_ref, rb_ref, eidx_ref, gate_ref, rank_ref, cnt_ref, base_ref):
    i = pl.program_id(0)

    @pl.when(i == 0)
    def _():
        base_ref[...] = jnp.zeros(base_ref.shape, F32)

    m = m_ref[0]
    h = x_ref[...] * (1.0 + m[4:5, :]) + m[3:4, :]
    h_hi = h.astype(BF16)
    h_lo = (h - h_hi.astype(F32)).astype(BF16)
    rw = rw_ref[...]
    rw_hi = rw.astype(BF16)
    rw_lo = (rw - rw_hi.astype(F32)).astype(BF16)
    logits = _dot_nt(rw_hi, h_hi) + (_dot_nt(rw_hi, h_lo) + _dot_nt(rw_lo, h_hi))
    scores = _sigmoid(logits)
    biased = scores + rb_ref[...]

    neg = -jnp.inf
    b3 = biased.reshape(N_GROUPS, GROUP_SIZE, T)
    io3 = lax.broadcasted_iota(I32, b3.shape, 1)
    m1 = jnp.max(b3, axis=1, keepdims=True)
    f1 = jnp.min(jnp.where(b3 == m1, io3, GROUP_SIZE), axis=1, keepdims=True)
    m2 = jnp.max(jnp.where(io3 == f1, neg, b3), axis=1, keepdims=True)
    gs = (m1 + m2).reshape(N_GROUPS, T)

    gio = lax.broadcasted_iota(I32, gs.shape, 0)
    gsel = jnp.zeros(gs.shape, F32)
    for _ in range(TOPK_GROUPS):
        mx = jnp.max(gs, axis=0, keepdims=True)
        f = jnp.min(jnp.where(gs == mx, gio, N_GROUPS), axis=0, keepdims=True)
        hit = gio == f
        gsel = jnp.where(hit, 1.0, gsel)
        gs = jnp.where(hit, neg, gs)
    masked = jnp.where(gsel.reshape(N_GROUPS, 1, T) > 0.5, b3, neg).reshape(N_EXPERTS, T)

    eio = lax.broadcasted_iota(I32, masked.shape, 0)
    hits, gates, eids = [], [], []
    onehot = jnp.zeros(masked.shape, F32)
    for _ in range(TOP_K):
        mx = jnp.max(masked, axis=0, keepdims=True)
        f = jnp.min(jnp.where(masked == mx, eio, N_EXPERTS), axis=0, keepdims=True)
        hit = eio == f
        hits.append(hit)
        eids.append(f)
        gates.append(jnp.sum(jnp.where(hit, scores, 0.0), axis=0, keepdims=True))
        onehot = jnp.where(hit, 1.0, onehot)
        masked = jnp.where(hit, neg, masked)
    gsum = gates[0]
    for gk in gates[1:]:
        gsum = gsum + gk

    r_io = lax.broadcasted_iota(I32, (T, T), 0)
    c_io = lax.broadcasted_iota(I32, (T, T), 1)
    upper = jnp.where(r_io < c_io, 1.0, 0.0).astype(BF16)
    before = base_ref[...] + _dot(onehot.astype(BF16), upper)
    for k in range(TOP_K):
        eidx_ref[k:k + 1, :] = eids[k]
        gate_ref[k:k + 1, :] = gates[k] / gsum * ROUTED_SCALE
        rank_ref[k:k + 1, :] = jnp.sum(jnp.where(hits[k], before, 0.0), axis=0, keepdims=True).astype(I32)
    base_new = base_ref[...] + jnp.sum(onehot, axis=1, keepdims=True)
    base_ref[...] = base_new
    cnt_ref[...] = jnp.broadcast_to(base_new, cnt_ref.shape)


def _router(x, mod, rw_t, rb_col, nlat):
    n = x.shape[0]
    kt = pl.BlockSpec((TOP_K, T), lambda i: (0, i))
    return pl.pallas_call(
        _router_kernel,
        grid=(n // T,),
        in_specs=[
            pl.BlockSpec((T, D), lambda i: (i, 0)),
            pl.BlockSpec((1, MOD_ROWS, D), lambda i: (i // nlat, 0, 0)),
            pl.BlockSpec((N_EXPERTS, D), lambda i: (0, 0)),
            pl.BlockSpec((N_EXPERTS, 1), lambda i: (0, 0)),
        ],
        out_specs=[kt, kt, kt, pl.BlockSpec((N_EXPERTS, 128), lambda i: (0, 0))],
        out_shape=[jax.ShapeDtypeStruct((TOP_K, n), I32), jax.ShapeDtypeStruct((TOP_K, n), F32),
                   jax.ShapeDtypeStruct((TOP_K, n), I32), jax.ShapeDtypeStruct((N_EXPERTS, 128), F32)],
        scratch_shapes=[pltpu.VMEM((N_EXPERTS, 1), F32)],
        compiler_params=_cparams(("arbitrary",)),
        name="moe_router",
    )(x, mod, rw_t, rb_col)


def _row_copy(src, src_row, dst, dst_row, sem):
    return pltpu.make_async_copy(src.at[pl.ds(src_row, 1), :], dst.at[pl.ds(dst_row, 1), :], sem)


def _pack_bf16_pair(v):
    half = v.shape[1] // 2
    lo = pltpu.bitcast(v[:, :half].astype(BF16).astype(F32), U32)
    hi = pltpu.bitcast(v[:, half:].astype(BF16).astype(F32), U32)
    return hi | (lo >> 16)


def _unpack_bf16_pair(w):
    lo = pltpu.bitcast(w << 16, F32)
    hi = pltpu.bitcast(w & jnp.uint32(0xFFFF0000), F32)
    return lo, hi


def _dispatch_kernel(dest_ref, x_ref, m_ref, xs_ref, h_ref, sem):
    i = pl.program_id(0)
    slot = i % 2
    m = m_ref[0]
    h_ref[slot] = _pack_bf16_pair(x_ref[...] * (1.0 + m[4:5, :]) + m[3:4, :])

    def body(tb, carry):
        for u in range(DMA_UNROLL):
            t = tb * DMA_UNROLL + u
            for k in range(TOP_K):
                _row_copy(h_ref.at[slot], t, xs_ref, dest_ref[0, k, t], sem.at[slot]).start(priority=k % 2)
        return carry

    lax.fori_loop(0, T // DMA_UNROLL, body, 0)

    def drain(s):
        for _ in range(TOP_K):
            pltpu.make_async_copy(h_ref.at[s], xs_ref.at[pl.ds(0, T), :], sem.at[s]).wait()

    @pl.when(i > 0)
    def _():
        drain(1 - slot)

    @pl.when(i == pl.num_programs(0) - 1)
    def _():
        drain(slot)


def _dispatch(dest3, x, mod, nlat):
    n = x.shape[0]
    return pl.pallas_call(
        _dispatch_kernel,
        grid=(n // T,),
        in_specs=[
            pl.BlockSpec((1, TOP_K, T), lambda i: (i, 0, 0), memory_space=pltpu.SMEM),
            pl.BlockSpec((T, D), lambda i: (i, 0)),
            pl.BlockSpec((1, MOD_ROWS, D), lambda i: (i // nlat, 0, 0)),
        ],
        out_specs=pl.BlockSpec(memory_space=pl.ANY),
        out_shape=jax.ShapeDtypeStruct((n * TOP_K, D // 2), U32),
        scratch_shapes=[pltpu.VMEM((2, T, D // 2), U32), pltpu.SemaphoreType.DMA((2,))],
        compiler_params=_cparams(("arbitrary",)),
        name="moe_dispatch",
    )(dest3, x, mod)


def _expert_kernel(vt_ref, ve_ref, vlo_ref, vhi_ref, vnew_ref, nv_ref, xs_ref, wg_ref, wu_ref, wd_ref,
                   ys_ref, wgu_bf, wd_bf):
    v = pl.program_id(0)

    @pl.when(v < nv_ref[0])
    def _():
        @pl.when(vnew_ref[v] == 1)
        def _():
            wgu_bf[:, 0:EXPERT_FF] = wg_ref[0, 0].astype(BF16)
            wgu_bf[:, EXPERT_FF:2 * EXPERT_FF] = wu_ref[0, 0].astype(BF16)
            wd_bf[...] = wd_ref[0, 0].astype(BF16)

        def ffn(r):
            x_lo, x_hi = _unpack_bf16_pair(xs_ref[r * EXPERT_RB:(r + 1) * EXPERT_RB, :])
            xb = jnp.concatenate([x_lo.astype(BF16), x_hi.astype(BF16)], axis=1)
            h12 = _dot(xb, wgu_bf[...])
            a = (_silu(h12[:, 0:EXPERT_FF]) * h12[:, EXPERT_FF:2 * EXPERT_FF]).astype(BF16)
            return _pack_bf16_pair(_dot(a, wd_bf[...]))

        lo = vlo_ref[v]
        hi = vhi_ref[v]
        whole = jnp.logical_and(lo == 0, hi == TS)

        @pl.when(whole)
        def _():
            for r in range(TS // EXPERT_RB):
                ys_ref[r * EXPERT_RB:(r + 1) * EXPERT_RB, :] = ffn(r)

        @pl.when(jnp.logical_not(whole))
        def _():
            for r in range(TS // EXPERT_RB):
                c0, c1 = r * EXPERT_RB, (r + 1) * EXPERT_RB
                blk = slice(c0, c1)

                @pl.when(jnp.logical_and(lo < c1, hi > c0))
                def _():
                    y = ffn(r)
                    row = c0 + lax.broadcasted_iota(I32, (EXPERT_RB, 1), 0)
                    mine = jnp.logical_and(row >= lo, row < hi)
                    starts_before = lo <= c0

                    @pl.when(jnp.logical_and(starts_before, hi >= c1))
                    def _():
                        ys_ref[blk, :] = y

                    @pl.when(jnp.logical_and(starts_before, hi < c1))
                    def _():
                        ys_ref[blk, :] = jnp.where(mine, y, jnp.uint32(0))

                    @pl.when(jnp.logical_not(starts_before))
                    def _():
                        ys_ref[blk, :] = jnp.where(mine, y, ys_ref[blk, :])


def _expert_ffn(sched, xs, w_gate, w_up, w_down, li):
    vt, ve, vlo, vhi, vnew, nv = sched
    nslots = xs.shape[0]
    nvis = vt.shape[0]
    w_in = lambda v, vt, ve, *_: (li, ve[v], 0, 0)
    rows = lambda v, vt, *_: (vt[v], 0)
    grid_spec = pltpu.PrefetchScalarGridSpec(
        num_scalar_prefetch=6,
        grid=(nvis,),
        in_specs=[
            pl.BlockSpec((TS, D // 2), rows),
            pl.BlockSpec((1, 1, D, EXPERT_FF), w_in),
            pl.BlockSpec((1, 1, D, EXPERT_FF), w_in),
            pl.BlockSpec((1, 1, EXPERT_FF, D), w_in),
        ],
        out_specs=pl.BlockSpec((TS, D // 2), rows),
        scratch_shapes=[pltpu.VMEM((D, 2 * EXPERT_FF), BF16), pltpu.VMEM((EXPERT_FF, D), BF16)],
    )
    return pl.pallas_call(
        _expert_kernel,
        grid_spec=grid_spec,
        out_shape=jax.ShapeDtypeStruct((nslots, D // 2), U32),
        compiler_params=_cparams(("arbitrary",)),
        name="moe_experts",
    )(vt, ve, vlo, vhi, vnew, nv, xs, w_gate, w_up, w_down)


def _combine_kernel(dest_ref, ys_ref, x_ref, m_ref, gt_ref, sg_ref, su_ref, sd_ref, g_ref, b_ref,
                    o_ref, *scratch):
    bufs, sem = scratch[:TOP_K], scratch[TOP_K]

    def body(tb, carry):
        for u in range(DMA_UNROLL):
            t = tb * DMA_UNROLL + u
            for k in range(TOP_K):
                _row_copy(ys_ref, dest_ref[0, k, t], bufs[k], t, sem).start(priority=k % 2)
        return carry

    lax.fori_loop(0, T // DMA_UNROLL, body, 0)

    m = m_ref[0]
    x = x_ref[...]
    hb = (x * (1.0 + m[4:5, :]) + m[3:4, :]).astype(BF16)
    f = _dot((_silu(_dot(hb, sg_ref[...])) * _dot(hb, su_ref[...])).astype(BF16), sd_ref[...])

    for k in range(TOP_K):
        pltpu.make_async_copy(ys_ref.at[pl.ds(0, T), :], bufs[k], sem).wait()
    gt = gt_ref[...]
    f_lo = f[:, :D // 2]
    f_hi = f[:, D // 2:]
    for k in range(TOP_K):
        y_lo, y_hi = _unpack_bf16_pair(bufs[k][...])
        f_lo = f_lo + gt[:, k:k + 1] * y_lo
        f_hi = f_hi + gt[:, k:k + 1] * y_hi
    f = jnp.concatenate([f_lo, f_hi], axis=1)
    o_ref[...] = _residual_ln(x, f, m, 5, g_ref[...], b_ref[...])


def _combine(dest3, ys, x, mod, gates_t, sg_bf, su_bf, sd_bf, g, b, nlat, n):
    const2 = lambda i: (0, 0)
    return pl.pallas_call(
        _combine_kernel,
        grid=(n // T,),
        in_specs=[
            pl.BlockSpec((1, TOP_K, T), lambda i: (i, 0, 0), memory_space=pltpu.SMEM),
            pl.BlockSpec(memory_space=pl.ANY),
            pl.BlockSpec((T, D), lambda i: (i, 0)),
            pl.BlockSpec((1, MOD_ROWS, D), lambda i: (i // nlat, 0, 0)),
            pl.BlockSpec((T, TOP_K), lambda i: (i, 0)),
            pl.BlockSpec((D, EXPERT_FF), const2),
            pl.BlockSpec((D, EXPERT_FF), const2),
            pl.BlockSpec((EXPERT_FF, D), const2),
            pl.BlockSpec((1, D), const2),
            pl.BlockSpec((1, D), const2),
        ],
        out_specs=pl.BlockSpec((T, D), lambda i: (i, 0)),
        out_shape=jax.ShapeDtypeStruct((n, D), F32),
        scratch_shapes=[pltpu.VMEM((T, D // 2), U32)] * TOP_K + [pltpu.SemaphoreType.DMA],
        compiler_params=_cparams(("arbitrary",)),
        name="moe_combine",
    )(dest3, ys, x, mod, gates_t, sg_bf, su_bf, sd_bf, g, b)


def _slot_schedule(counts, n_slot_tiles):
    offs = jnp.cumsum(counts) - counts
    ends = offs + counts
    first_tile = offs // TS
    last_tile = jnp.maximum(ends - 1, offs) // TS
    nvis_e = jnp.where(counts > 0, last_tile - first_tile + 1, 0)
    vstart = jnp.cumsum(nvis_e) - nvis_e
    total = jnp.sum(nvis_e)
    nvis = n_slot_tiles + N_EXPERTS - 1
    v = jnp.arange(nvis, dtype=I32)
    experts = jnp.arange(N_EXPERTS, dtype=I32)
    e = jnp.minimum(jnp.sum((vstart + nvis_e)[None, :] <= v[:, None], axis=1), N_EXPERTS - 1).astype(I32)
    onehot = e[:, None] == experts[None, :]
    pick = lambda a: jnp.sum(jnp.where(onehot, a[None, :], 0), axis=1)
    tile = pick(first_tile) + (v - pick(vstart))
    lo = jnp.maximum(pick(offs), tile * TS) - tile * TS
    hi = jnp.minimum(pick(ends), (tile + 1) * TS) - tile * TS
    valid = v < total
    e_last = jnp.max(jnp.where(counts > 0, experts, 0))
    vt = jnp.where(valid, tile, n_slot_tiles - 1).astype(I32)
    ve = jnp.where(valid, e, e_last).astype(I32)
    vlo = jnp.where(valid, lo, 0).astype(I32)
    vhi = jnp.where(valid, hi, 0).astype(I32)
    vnew = jnp.concatenate([jnp.ones((1,), I32), (ve[1:] != ve[:-1]).astype(I32)])
    return offs, (vt, ve, vlo, vhi, vnew, total.astype(I32).reshape(1))


def _moe_layer(x, mod, rw_t, rb_col, w_gate, w_up, w_down, li, sg_bf, su_bf, sd_bf, g, b, nlat, n_out):
    n = x.shape[0]
    eidx, gates, rank, cnt = _router(x, mod, rw_t, rb_col, nlat)
    counts = cnt[:, 0].astype(I32)
    offs, sched = _slot_schedule(counts, n * TOP_K // TS)
    experts = jnp.arange(N_EXPERTS, dtype=I32)
    dest = rank + jnp.sum(jnp.where(eidx[..., None] == experts, offs, 0), axis=-1)
    dest3 = dest.reshape(TOP_K, n // T, T).transpose(1, 0, 2)
    xs = _dispatch(dest3, x, mod, nlat)
    ys = _expert_ffn(sched, xs, w_gate, w_up, w_down, li)
    return _combine(dest3, ys, x, mod, gates.T, sg_bf, su_bf, sd_bf, g, b, nlat, n_out)


def _rope_tables(n_lat, n_ctx):
    rows = n_lat // GRID_W
    r, cidx = jnp.meshgrid(jnp.arange(rows), jnp.arange(GRID_W), indexing="ij")
    pos = jnp.stack([r.reshape(-1), cidx.reshape(-1)], axis=-1).astype(F32)
    nf = DIFF_QK_DIM // 4
    inv = ROPE_BASE ** (-jnp.arange(nf, dtype=F32) / nf)
    ang = jnp.broadcast_to(pos[:, :, None, None] * inv, (n_lat, 2, 2, nf)).reshape(n_lat, DIFF_QK_DIM)
    cos = jnp.concatenate([jnp.cos(ang), jnp.ones((n_ctx, DIFF_QK_DIM), F32)], axis=0)
    sin = jnp.concatenate([jnp.sin(ang), jnp.zeros((n_ctx, DIFF_QK_DIM), F32)], axis=0)
    return cos, sin


def kernel(x, c, ctx, c_ctx, ada_w, ada_b, ln_g, ln_b, ev_w_in, ev_w_out, pool_w, pool_scale, sgu_ln_g, sgu_ln_b, sgu_w, sgu_b, od_w_in, od_w_out, diff_lambda, diff_subln, conv_w, router_w, router_bias, exp_w_gate, exp_w_up, exp_w_down, sh_w_gate, sh_w_up, sh_w_down):
    bsz, n_lat, d = x.shape
    n_ctx = ctx.shape[1]
    assert bsz == 1 and d == D and n_lat % T == 0 and n_ctx % T == 0
    assert n_lat % ATT_TK == 0 and n_lat % ATT_TQ == 0 and n_lat % n_ctx == 0 and n_ctx % LANES == 0
    depth = ada_w.shape[0]
    nlat = n_lat // T

    xs = jnp.concatenate([x[0], ctx[0]], axis=0)
    mods = _ada_mods(jnp.stack([c[0], c_ctx], axis=1), ada_w, ada_b)
    cos, sin = _rope_tables(n_lat, n_ctx)
    eye = jnp.eye(T // SGU_CHUNK, dtype=F32)

    for li in range(depth):
        j = li // 2
        mod = mods[li]
        g0, b0 = ln_g[li, 0][None], ln_b[li, 0][None]
        g1, b1 = ln_g[li, 1][None], ln_b[li, 1][None]
        if li % 2 == 0:
            p = _mod_matmul(xs, mod, ev_w_in[j].astype(BF16), nlat, sh=0, sc=1, tn=EVEN_IN)
            sguw_bd = jnp.einsum("ab,hpq->hapbq", eye, sgu_w[j]).reshape(SGU_HEADS, T, T).astype(BF16)
            sgub_t = jnp.tile(sgu_b[j].T, (T // SGU_CHUNK, 1))
            xs = _even_mix(p, xs, mod, pool_w[j].astype(BF16), pool_scale[j], sgu_ln_g[j][None], sgu_ln_b[j][None],
                           sguw_bd, sgub_t, ev_w_out[j].astype(BF16), g0, b0, nlat)
        else:
            q, k, v = _qkv_proj(xs, mod, od_w_in[j][:, :3 * DIFF_WIDTH].astype(BF16), cos, sin, nlat)
            p = _mod_matmul(xs, mod, od_w_in[j][:, 3 * DIFF_WIDTH:].astype(BF16), nlat, sh=0, sc=1,
                            tn=3 * CONV_WIDTH)
            lam_init = 0.8 - 0.6 * math.exp(-0.3 * li)
            on = _diff_attention(q, k, v, diff_lambda[j], diff_subln[j][None], n_lat, lam_init)
            xs = _odd_out(on, p, conv_w[j], xs, mod, od_w_out[j].astype(BF16), g0, b0, nlat)
        xs = _moe_layer(xs, mod, router_w[li].T, router_bias[li][:, None], exp_w_gate, exp_w_up,
                        exp_w_down, li, sh_w_gate[li].astype(BF16), sh_w_up[li].astype(BF16),
                        sh_w_down[li].astype(BF16), g1, b1, nlat,
                        n_out=n_lat if li == depth - 1 else n_lat + n_ctx)
    return xs[None]
```
